```python
import jax, jax.numpy as jnp
from jax import lax
import numpy as np

D_MODEL = 4096
BATCH = 4
SEQ = 2048
DEPTH = 2
DEC_BATCH = 128
DEC_SEQ = 4
PAST_LEN = 16384
PAGE_SIZE = 128

N_A_LAYERS = DEPTH // 2
N_B_LAYERS = DEPTH - N_A_LAYERS
HEAD_A = 64
N_HEADS_A = D_MODEL // HEAD_A
DECAY_LORA = max(32, int(round(1.8 * D_MODEL ** 0.5 / 32)) * 32)
ICLR_LORA = max(32, int(round(1.8 * D_MODEL ** 0.5 / 32)) * 32)
GATE_LORA = max(32, int(round(0.6 * D_MODEL ** 0.8 / 32)) * 32)
GN_EPS = 64e-5
N_HEADS_B = 64
Q_LORA = 1024
KV_LORA = 512
NOPE_DIM = 128
ROPE_DIM = 64
V_DIM = 128
ROPE_THETA = 10000.0
SOFTMAX_SCALE = (NOPE_DIM + ROPE_DIM) ** -0.5
Q_BLOCK = 128
D_FF = -(-8 * D_MODEL // (3 * 256)) * 256
RMS_EPS = 1e-6

kernel_name = 'yoco_rwkv7_mla_decode_step'


def rms_norm(x, g):
    xf = x.astype(jnp.float32)
    y = xf * lax.rsqrt(jnp.mean(xf * xf, axis=-1, keepdims=True) + RMS_EPS)
    return (y * g.astype(jnp.float32)).astype(x.dtype)


def rope(x, pos):
    half = x.shape[-1] // 2
    inv = ROPE_THETA ** (-jnp.arange(half, dtype=jnp.float32) / half)
    ang = pos.astype(jnp.float32)[:, None] * inv[None, :]
    cos, sin = jnp.cos(ang)[:, None, :], jnp.sin(ang)[:, None, :]
    x1, x2 = x[..., :half].astype(jnp.float32), x[..., half:].astype(jnp.float32)
    return jnp.concatenate([x1 * cos - x2 * sin, x1 * sin + x2 * cos], axis=-1).astype(x.dtype)


def swiglu(h, w_gate, w_up, w_down):
    return (jax.nn.silu(h @ w_gate) * (h @ w_up)) @ w_down


def rwkv7_time_mix(h, shift_prev, s0, p, i):
    B, T, _ = h.shape
    mu = p['rw_mu'][i]
    h_prev = jnp.concatenate([shift_prev[:, None, :].astype(h.dtype), h[:, :-1]], axis=1)
    dx = h_prev - h
    mix = lambda j: h + dx * mu[j]
    r = mix(0) @ p['rw_w_r'][i]
    w_log = -jax.nn.softplus(-(p['rw_w0'][i] + jnp.tanh(mix(1) @ p['rw_w1'][i]) @ p['rw_w2'][i])) - 0.5
    k = mix(2) @ p['rw_w_k'][i]
    v = mix(3) @ p['rw_w_v'][i]
    a = jax.nn.sigmoid(p['rw_a0'][i] + (mix(4) @ p['rw_a1'][i]) @ p['rw_a2'][i])
    g = jax.nn.sigmoid(mix(5) @ p['rw_g1'][i]) @ p['rw_g2'][i]
    heads = lambda t: t.reshape(B, T, N_HEADS_A, HEAD_A).astype(jnp.float32)
    kk = heads(k * p['rw_k_k'][i])
    kk = kk * lax.rsqrt(jnp.maximum(jnp.sum(kk * kk, axis=-1, keepdims=True), 1e-24))
    k = k * (1.0 + (a - 1.0) * p['rw_k_a'][i])
    decay = jnp.exp(-jnp.exp(w_log.astype(jnp.float32)))
    r_h, k_h, v_h, a_h, d_h = heads(r), heads(k), heads(v), heads(a), heads(decay)

    def step(S, inp):
        r_t, d_t, k_t, v_t, kk_t, a_t = inp
        sa = jnp.einsum('bhvk,bhk->bhv', S, -kk_t)
        S = (S * d_t[:, :, None, :] + sa[..., None] * (kk_t * a_t)[:, :, None, :]
             + v_t[..., None] * k_t[:, :, None, :])
        return S, jnp.einsum('bhvk,bhk->bhv', S, r_t)

    xs = tuple(jnp.moveaxis(t, 1, 0) for t in (r_h, d_h, k_h, v_h, kk, a_h))
    s_T, o = lax.scan(step, s0.astype(jnp.float32), xs)
    o = jnp.moveaxis(o, 0, 1)
    mean = jnp.mean(o, axis=-1, keepdims=True)
    var = jnp.mean(jnp.square(o - mean), axis=-1, keepdims=True)
    o_n = ((o - mean) * lax.rsqrt(var + GN_EPS)).reshape(B, T, D_MODEL)
    o_n = o_n * p['rw_ln_w'][i].astype(jnp.float32) + p['rw_ln_b'][i].astype(jnp.float32)
    bonus = jnp.sum(r_h * k_h * p['rw_r_k'][i].astype(jnp.float32), axis=-1, keepdims=True) * v_h
    y = (o_n + bonus.reshape(B, T, D_MODEL)).astype(h.dtype)
    return (y * g) @ p['rw_w_o'][i], s_T


def shared_latent_kv(x, pos, p):
    kv = rms_norm(x, p['kv_g_in']) @ p['kv_w_a']
    ckv = rms_norm(kv[..., :KV_LORA], p['kv_g_a'])
    kpe = rope(kv[..., None, KV_LORA:], pos)[:, :, 0, :]
    return ckv, kpe


def mla_layer(h, pos, ckv, kpe, attend, p, j):
    B, T, _ = h.shape
    q = (rms_norm(h @ p['q_w_a'][j], p['q_g_a'][j]) @ p['q_w_b'][j]).reshape(B, T, N_HEADS_B, NOPE_DIM + ROPE_DIM)
    q_nope, q_pe = q[..., :NOPE_DIM], rope(q[..., NOPE_DIM:], pos)
    w_b = p['kv_w_b']
    q_lat = jnp.einsum('bthd,chd->bthc', q_nope, w_b[..., :NOPE_DIM])
    o_lat = attend(q_lat, q_pe, ckv, kpe)
    o = jnp.einsum('bthc,che->bthe', o_lat, w_b[..., NOPE_DIM:])
    return o.reshape(B, T, N_HEADS_B * V_DIM) @ p['mla_w_o'][j]


def causal_block_attend(q_lat, q_pe, ckv, kpe):
    B, T, H, C = q_lat.shape
    nb = T // Q_BLOCK
    qb = lambda t: jnp.moveaxis(t.reshape(B, nb, Q_BLOCK, *t.shape[2:]), 1, 0)
    k_pos = jnp.arange(T)

    def block(args):
        i, ql, qp = args
        s = (jnp.einsum('bqhc,bkc->bhqk', ql, ckv)
             + jnp.einsum('bqhr,bkr->bhqk', qp, kpe)).astype(jnp.float32) * SOFTMAX_SCALE
        q_pos = i * Q_BLOCK + jnp.arange(Q_BLOCK)
        s = jnp.where(k_pos[None, :] <= q_pos[:, None], s, -jnp.inf)
        pr = jax.nn.softmax(s, axis=-1).astype(ckv.dtype)
        return jnp.einsum('bhqk,bkc->bqhc', pr, ckv)

    o = lax.map(block, (jnp.arange(nb), qb(q_lat), qb(q_pe)))
    return jnp.moveaxis(o, 0, 1).reshape(B, T, H, C)


def paged_latent_attend(q_lat, q_pe, ckv_new, kpe_new, cache_ckv, cache_kpe, page_table):
    Bd, Tn, H, C = q_lat.shape

    def scores(kc, kr):
        return (jnp.einsum('bqhc,bkc->bhqk', q_lat, kc)
                + jnp.einsum('bqhr,bkr->bhqk', q_pe, kr)).astype(jnp.float32) * SOFTMAX_SCALE

    def update(carry, s, vals):
        m, l, acc = carry
        m_new = jnp.maximum(m, jnp.max(s, axis=-1))
        corr = jnp.exp(m - m_new)
        pr = jnp.exp(s - m_new[..., None])
        acc = acc * corr[..., None] + jnp.einsum('bhqk,bkc->bhqc', pr, vals.astype(jnp.float32))
        return (m_new, l * corr + jnp.sum(pr, axis=-1), acc)

    def page_step(carry, phys):
        kc, kr = cache_ckv[phys], cache_kpe[phys]
        return update(carry, scores(kc, kr), kc), None

    init = (jnp.full((Bd, H, Tn), -jnp.inf, jnp.float32),
            jnp.zeros((Bd, H, Tn), jnp.float32),
            jnp.zeros((Bd, H, Tn, C), jnp.float32))
    carry, _ = lax.scan(page_step, init, page_table.T)
    causal = jnp.tril(jnp.ones((Tn, Tn), dtype=bool))
    s_new = jnp.where(causal, scores(ckv_new, kpe_new), -jnp.inf)
    _, l, acc = update(carry, s_new, ckv_new)
    return jnp.moveaxis(acc / l[..., None], 1, 2).astype(q_lat.dtype)


def trunk(x, pos, wkv0, shift0, attend, p):
    wkv_out, shift_out = [], []
    ckv = kpe = None
    for l in range(DEPTH):
        h = rms_norm(x, p['g_mix_pre'][l])
        if l < N_A_LAYERS:
            mix, s_new = rwkv7_time_mix(h, shift0[l], wkv0[l], p, l)
            wkv_out.append(s_new)
            shift_out.append(h[:, -1])
        else:
            if l == N_A_LAYERS:
                ckv, kpe = shared_latent_kv(x, pos, p)
            mix = mla_layer(h, pos, ckv, kpe, attend, p, l - N_A_LAYERS)
        x = x + rms_norm(mix, p['g_mix_post'][l])
        h = rms_norm(x, p['g_ffn_pre'][l])
        x = x + rms_norm(swiglu(h, p['ffn_w_gate'][l], p['ffn_w_up'][l], p['ffn_w_down'][l]), p['g_ffn_post'][l])
    return x, jnp.stack(wkv_out), jnp.stack(shift_out), ckv, kpe


def setup_inputs(seed: int = 0) -> dict:
    key = jax.random.key(seed)
    ks = iter(jax.random.split(key, 64))
    f32 = jnp.float32
    nrm = lambda shape, scale: jax.random.normal(next(ks), shape, f32) * scale
    uni = lambda shape, lo, hi: jax.random.uniform(next(ks), shape, f32, lo, hi)
    gain = lambda shape: 1.0 + nrm(shape, 0.05)
    D, F, A, Bn = D_MODEL, D_FF, N_A_LAYERS, N_B_LAYERS
    n_pages = PAST_LEN // PAGE_SIZE
    n_used = DEC_BATCH * n_pages
    n_pool = n_used + n_used // 4
    inp = {}
    inp['x_prompt'] = nrm((BATCH, SEQ, D), 1.0)
    inp['x_sample'] = nrm((DEC_BATCH, DEC_SEQ, D), 1.0)
    inp['state_wkv'] = nrm((A, DEC_BATCH, N_HEADS_A, HEAD_A, HEAD_A), 0.3)
    inp['state_shift'] = nrm((A, DEC_BATCH, D), 1.0)
    inp['cache_ckv'] = nrm((n_pool, PAGE_SIZE, KV_LORA), 1.0)
    inp['cache_kpe'] = nrm((n_pool, PAGE_SIZE, ROPE_DIM), 1.0)
    inp['page_table'] = jax.random.permutation(next(ks), n_pool)[:n_used].reshape(DEC_BATCH, n_pages).astype(jnp.int32)
    inp['g_mix_pre'] = gain((DEPTH, D))
    inp['g_mix_post'] = gain((DEPTH, D))
    inp['g_ffn_pre'] = gain((DEPTH, D))
    inp['g_ffn_post'] = gain((DEPTH, D))
    inp['rw_mu'] = uni((A, 6, D), 0.0, 1.0)
    inp['rw_w_r'] = nrm((A, D, D), D ** -0.5)
    inp['rw_w_k'] = nrm((A, D, D), D ** -0.5)
    inp['rw_w_v'] = nrm((A, D, D), D ** -0.5)
    inp['rw_w_o'] = nrm((A, D, D), D ** -0.5)
    inp['rw_w0'] = uni((A, D), -5.0, 0.0)
    inp['rw_w1'] = nrm((A, D, DECAY_LORA), D ** -0.5)
    inp['rw_w2'] = nrm((A, DECAY_LORA, D), 0.5 * DECAY_LORA ** -0.5)
    inp['rw_a0'] = nrm((A, D), 0.1)
    inp['rw_a1'] = nrm((A, D, ICLR_LORA), D ** -0.5)
    inp['rw_a2'] = nrm((A, ICLR_LORA, D), 0.5 * ICLR_LORA ** -0.5)
    inp['rw_g1'] = nrm((A, D, GATE_LORA), D ** -0.5)
    inp['rw_g2'] = nrm((A, GATE_LORA, D), GATE_LORA ** -0.5)
    inp['rw_k_k'] = 0.85 + nrm((A, D), 0.05)
    inp['rw_k_a'] = gain((A, D))
    inp['rw_r_k'] = nrm((A, N_HEADS_A, HEAD_A), 0.1)
    inp['rw_ln_w'] = gain((A, D))
    inp['rw_ln_b'] = nrm((A, D), 0.01)
    inp['kv_g_in'] = gain((D,))
    inp['kv_w_a'] = nrm((D, KV_LORA + ROPE_DIM), D ** -0.5)
    inp['kv_g_a'] = gain((KV_LORA,))
    inp['kv_w_b'] = nrm((KV_LORA, N_HEADS_B, NOPE_DIM + V_DIM), KV_LORA ** -0.5)
    inp['q_w_a'] = nrm((Bn, D, Q_LORA), D ** -0.5)
    inp['q_g_a'] = gain((Bn, Q_LORA))
    inp['q_w_b'] = nrm((Bn, Q_LORA, N_HEADS_B * (NOPE_DIM + ROPE_DIM)), Q_LORA ** -0.5)
    inp['mla_w_o'] = nrm((Bn, N_HEADS_B * V_DIM, D), (N_HEADS_B * V_DIM) ** -0.5)
    inp['ffn_w_gate'] = nrm((DEPTH, D, F), D ** -0.5)
    inp['ffn_w_up'] = nrm((DEPTH, D, F), D ** -0.5)
    inp['ffn_w_down'] = nrm((DEPTH, F, D), F ** -0.5)
    return inp


def reference(x_prompt, x_sample, state_wkv, state_shift, cache_ckv, cache_kpe, page_table,
              g_mix_pre, g_mix_post, g_ffn_pre, g_ffn_post,
              rw_mu, rw_w_r, rw_w_k, rw_w_v, rw_w_o, rw_w0, rw_w1, rw_w2, rw_a0, rw_a1, rw_a2,
              rw_g1, rw_g2, rw_k_k, rw_k_a, rw_r_k, rw_ln_w, rw_ln_b,
              kv_g_in, kv_w_a, kv_g_a, kv_w_b, q_w_a, q_g_a, q_w_b, mla_w_o,
              ffn_w_gate, ffn_w_up, ffn_w_down):
    p = dict(g_mix_pre=g_mix_pre, g_mix_post=g_mix_post, g_ffn_pre=g_ffn_pre, g_ffn_post=g_ffn_post,
             rw_mu=rw_mu, rw_w_r=rw_w_r, rw_w_k=rw_w_k, rw_w_v=rw_w_v, rw_w_o=rw_w_o,
             rw_w0=rw_w0, rw_w1=rw_w1, rw_w2=rw_w2, rw_a0=rw_a0, rw_a1=rw_a1, rw_a2=rw_a2,
             rw_g1=rw_g1, rw_g2=rw_g2, rw_k_k=rw_k_k, rw_k_a=rw_k_a, rw_r_k=rw_r_k,
             rw_ln_w=rw_ln_w, rw_ln_b=rw_ln_b,
             kv_g_in=kv_g_in, kv_w_a=kv_w_a, kv_g_a=kv_g_a, kv_w_b=kv_w_b,
             q_w_a=q_w_a, q_g_a=q_g_a, q_w_b=q_w_b, mla_w_o=mla_w_o,
             ffn_w_gate=ffn_w_gate, ffn_w_up=ffn_w_up, ffn_w_down=ffn_w_down)
    Bp, Tp, _ = x_prompt.shape
    pos_p = jnp.arange(Tp)
    wkv0_p = jnp.zeros((N_A_LAYERS, Bp, N_HEADS_A, HEAD_A, HEAD_A), jnp.float32)
    shift0_p = jnp.zeros((N_A_LAYERS, Bp, D_MODEL), x_prompt.dtype)
    y_p, wkv_p, shift_p, ckv_p, kpe_p = trunk(x_prompt, pos_p, wkv0_p, shift0_p, causal_block_attend, p)
    past_len = page_table.shape[1] * PAGE_SIZE
    pos_s = past_len + jnp.arange(x_sample.shape[1])
    attend_s = lambda ql, qp, ckv_new, kpe_new: paged_latent_attend(ql, qp, ckv_new, kpe_new, cache_ckv, cache_kpe, page_table)
    y_s, wkv_s, shift_s, ckv_s, kpe_s = trunk(x_sample, pos_s, state_wkv, state_shift, attend_s, p)
    return (y_p, y_s, wkv_p, shift_p, ckv_p, kpe_p, wkv_s, shift_s, ckv_s, kpe_s)
```

```python
import functools

import jax
import jax.numpy as jnp
import numpy as np
from jax import lax
from jax.experimental import pallas as pl
from jax.experimental.pallas import tpu as pltpu

F32 = jnp.float32
BF16 = jnp.bfloat16

V7X_VMEM_BYTES = 64 * 1024 * 1024
V7X_MXU_DIM = 256
LANES = 128
VMEM_LIMIT = V7X_VMEM_BYTES - 8 * 1024 * 1024

HEAD_A = 64
GN_EPS = 64e-5
RMS_EPS = 1e-6
ROPE_THETA = 10000.0
PAGE_SIZE = 128
NEG_INF = float("-inf")


def _cparams(sem):
    return pltpu.CompilerParams(dimension_semantics=sem, vmem_limit_bytes=VMEM_LIMIT)


def _pick(n, cands):
    for c in cands:
        if n % c == 0:
            return c
    return n


def _rms(x, g):
    return x * lax.rsqrt(jnp.mean(x * x, axis=-1, keepdims=True) + RMS_EPS) * g


def _norm_kernel(*refs, has_add, n_pre, pre_dtypes):
    it = iter(refs)
    x_ref = next(it)
    if has_add:
        mix_ref = next(it)
        gpost_ref = next(it)
    gpre_refs = [next(it) for _ in range(n_pre)]
    x = x_ref[...]
    if has_add:
        x = x + _rms(mix_ref[...], gpost_ref[...])
        xo_ref = next(it)
        xo_ref[...] = x
    if n_pre:
        xn = x * lax.rsqrt(jnp.mean(x * x, axis=-1, keepdims=True) + RMS_EPS)
        for g_ref, dt in zip(gpre_refs, pre_dtypes):
            o_ref = next(it)
            o_ref[...] = (xn * g_ref[...]).astype(dt)


def norm_call(x, mix=None, g_post=None, pre_gains=(), pre_dtypes=()):
    M, D = x.shape
    tr = _pick(M, (256, 128, 64, 32, 16, 8))
    row = pl.BlockSpec((tr, D), lambda i: (i, 0))
    vec = pl.BlockSpec((1, D), lambda i: (0, 0))
    has_add = mix is not None
    ins, in_specs = [x], [row]
    if has_add:
        ins += [mix, g_post.reshape(1, D)]
        in_specs += [row, vec]
    for g in pre_gains:
        ins.append(g.reshape(1, D))
        in_specs.append(vec)
    out_shape, out_specs = [], []
    if has_add:
        out_shape.append(jax.ShapeDtypeStruct((M, D), F32))
        out_specs.append(row)
    for dt in pre_dtypes:
        out_shape.append(jax.ShapeDtypeStruct((M, D), dt))
        out_specs.append(row)
    return pl.pallas_call(
        functools.partial(_norm_kernel, has_add=has_add, n_pre=len(pre_gains), pre_dtypes=tuple(pre_dtypes)),
        grid=(M // tr,), in_specs=in_specs, out_specs=out_specs, out_shape=out_shape,
        compiler_params=_cparams(("parallel",)),
    )(*ins)


def _mix_kernel(h_ref, hp_ref, mu_ref, *o_refs):
    h = h_ref[...]
    dx = hp_ref[...] - h
    for j, o_ref in enumerate(o_refs):
        o_ref[...] = (h + dx * mu_ref[j:j + 1, :]).astype(BF16)


def mix_call(h, hp, mu):
    M, D = h.shape
    n = mu.shape[0]
    tr = _pick(M, (256, 128, 64, 32, 16, 8))
    row = pl.BlockSpec((tr, D), lambda i: (i, 0))
    return pl.pallas_call(
        _mix_kernel, grid=(M // tr,),
        in_specs=[row, row, pl.BlockSpec((n, D), lambda i: (0, 0))],
        out_specs=[row] * n, out_shape=[jax.ShapeDtypeStruct((M, D), BF16)] * n,
        compiler_params=_cparams(("parallel",)),
    )(h, hp, mu)


def _matmul_kernel(a_ref, *refs, nb, ne, epilogue):
    b_refs, e_refs, o_refs = refs[:nb], refs[nb:nb + ne], refs[nb + ne:]
    a = a_ref[...]
    accs = [jnp.dot(a, b[...], preferred_element_type=F32) for b in b_refs]
    epilogue(accs, e_refs, o_refs)


def _ep_store(accs, e_refs, o_refs):
    o_refs[0][...] = accs[0].astype(o_refs[0].dtype)


def _ep_swiglu(accs, e_refs, o_refs):
    g, u = accs
    o_refs[0][...] = (g * jax.nn.sigmoid(g) * u).astype(o_refs[0].dtype)


def matmul_call(a, bs, *, tm, tn, epilogue=_ep_store, extras=(), outs=None):
    M, K = a.shape
    N = bs[0].shape[1]
    tm, tn = min(tm, M), min(tn, N)
    assert M % tm == 0 and N % tn == 0, (M, N, tm, tn)
    if outs is None:
        outs = [((M, N), F32, (tm, tn), lambda i, j: (i, j))]
    in_specs = [pl.BlockSpec((tm, K), lambda i, j: (i, 0))]
    in_specs += [pl.BlockSpec((K, tn), lambda i, j: (0, j)) for _ in bs]
    in_specs += [pl.BlockSpec(bshape, imap) for _, bshape, imap in extras]
    return pl.pallas_call(
        functools.partial(_matmul_kernel, nb=len(bs), ne=len(extras), epilogue=epilogue),
        grid=(M // tm, N // tn), in_specs=in_specs,
        out_specs=[pl.BlockSpec(bshape, imap) for _, _, bshape, imap in outs],
        out_shape=[jax.ShapeDtypeStruct(shape, dt) for shape, dt, _, _ in outs],
        compiler_params=_cparams(("parallel", "arbitrary")),
    )(a, *bs, *[e[0] for e in extras])


def _lora_kernel(a_ref, w1_ref, w2_ref, bias_ref, o_ref, *, kind):
    z = jnp.dot(a_ref[...], w1_ref[...], preferred_element_type=F32)
    if kind == "decay":
        z = jnp.tanh(z)
    elif kind == "gate":
        z = jax.nn.sigmoid(z)
    y = jnp.dot(z.astype(BF16), w2_ref[...], preferred_element_type=F32)
    if kind == "decay":
        y = -jnp.exp(-jax.nn.softplus(-(bias_ref[...] + y)) - 0.5)
    elif kind == "iclr":
        y = jax.nn.sigmoid(bias_ref[...] + y)
    o_ref[...] = y


def lora_call(a, w1, w2, bias, kind):
    M, D = a.shape
    R = w1.shape[1]
    tm = _pick(M, (512, 256, 128, 64, 32, 16, 8))
    return pl.pallas_call(
        functools.partial(_lora_kernel, kind=kind), grid=(M // tm,),
        in_specs=[pl.BlockSpec((tm, D), lambda i: (i, 0)), pl.BlockSpec((D, R), lambda i: (0, 0)),
                  pl.BlockSpec((R, D), lambda i: (0, 0)), pl.BlockSpec((1, D), lambda i: (0, 0))],
        out_specs=pl.BlockSpec((tm, D), lambda i: (i, 0)),
        out_shape=jax.ShapeDtypeStruct((M, D), F32),
        compiler_params=_cparams(("parallel",)),
    )(a, w1, w2, bias.reshape(1, D))


def _split2(x):
    hi = x.astype(BF16)
    lo = (x - hi.astype(F32)).astype(BF16)
    return hi, lo


def _split3(x):
    h1 = x.astype(BF16)
    r1 = x - h1.astype(F32)
    h2 = r1.astype(BF16)
    h3 = (r1 - h2.astype(F32)).astype(BF16)
    return h1, h2, h3


_NN = (((1,), (0,)), ((), ()))
_NT = (((1,), (1,)), ((), ()))
_TN = (((0,), (0,)), ((), ()))


def _dg(a, b, dims):
    return lax.dot_general(a, b, dims, preferred_element_type=F32)


def _mm(a, b, dims, passes):
    if passes == 1:
        return _dg(a.astype(BF16), b.astype(BF16), dims)
    ah, al = _split2(a)
    bh, bl = _split2(b)
    return _dg(ah, bh, dims) + (_dg(ah, bl, dims) + _dg(al, bh, dims))


def _mm_exact_rhs(a, b_bf16, dims):
    h1, h2, h3 = _split3(a)
    return _dg(h1, b_bf16, dims) + (_dg(h2, b_bf16, dims) + _dg(h3, b_bf16, dims))


def _iota2(shape, dim):
    return lax.broadcasted_iota(jnp.int32, shape, dim)


def _wkv_kernel(r_ref, k_ref, v_ref, lw_ref, a_ref, g_ref, kk_ref, ka_ref, rk_ref, lnw_ref, lnb_ref,
                s0_ref, y_ref, st_ref, sbd_ref, *, C, passes):
    Lw = r_ref.shape[1]
    nsub = Lw // 256
    G = Lw // HEAD_A
    W2 = G * C
    assert W2 == 256 * (W2 // 256) and W2 % 256 == 0
    c_idx = pl.program_id(2)
    n_chunks = pl.num_programs(2)

    bd64 = (_iota2((256, 256), 0) // HEAD_A) == (_iota2((256, 256), 1) // HEAD_A)
    ones_bd = jnp.where(bd64, 1.0, 0.0).astype(BF16)
    tile_e = jnp.where((_iota2((HEAD_A, 256), 1) % HEAD_A) == _iota2((HEAD_A, 256), 0), 1.0, 0.0).astype(BF16)

    @pl.when(c_idx == 0)
    def _():
        for q in range(nsub):
            rows = s0_ref[0, 4 * q:4 * q + 4].reshape(256, HEAD_A)
            tiled = _mm_exact_rhs(rows, tile_e, _NN)
            sbd_ref[q] = jnp.where(bd64, tiled, 0.0)

    def segsum(x):
        outs = [_mm_exact_rhs(x[:, 256 * q:256 * (q + 1)], ones_bd, _NN) for q in range(nsub)]
        return outs[0] if nsub == 1 else jnp.concatenate(outs, axis=1)

    r, k, v, lw, a, g = r_ref[...], k_ref[...], v_ref[...], lw_ref[...], a_ref[...], g_ref[...]
    kk = k * kk_ref[...]
    kk = kk * lax.rsqrt(jnp.maximum(segsum(kk * kk), 1e-24))
    kmod = k * (1.0 + (a - 1.0) * ka_ref[...])
    b = kk * a

    tri = jnp.where(_iota2((C, C), 1) <= _iota2((C, C), 0), 1.0, 0.0).astype(BF16)
    l1, l2, l3 = _split3(lw)
    L = _dg(tri, l1, _NN) + (_dg(tri, l2, _NN) + _dg(tri, l3, _NN))
    Ltot = L[C - 1:C, :]
    eL = jnp.exp(L)
    enL = jnp.exp(-L)
    eR = jnp.exp(Ltot - L)
    at = -kk * jnp.exp(L - lw)
    bt = b * enL
    kt = kmod * enL
    rt = r * eL
    bh = b * eR
    kh = kmod * eR

    hmask = (_iota2((W2, Lw), 0) // C) == (_iota2((W2, Lw), 1) // HEAD_A)
    smask = (_iota2((W2, W2), 0) // C) == (_iota2((W2, W2), 1) // C)

    def bd_h(x):
        return jnp.where(hmask, jnp.concatenate([x] * G, axis=0), 0.0)

    def bd_s(x):
        return jnp.where(smask, jnp.concatenate([x] * G, axis=0), 0.0)

    t_i = _iota2((C, W2), 0)
    s_i = _iota2((C, W2), 1) % C
    lhs2 = jnp.concatenate([at, rt], axis=0)
    p_b = _mm(lhs2, bd_h(bt), _NT, passes)
    p_k = _mm(lhs2, bd_h(kt), _NT, passes)
    a_ab = jnp.where(s_i < t_i, p_b[:C], 0.0)
    a_rb = jnp.where(s_i <= t_i, p_b[C:], 0.0)
    a_ak = jnp.where(s_i < t_i, p_k[:C], 0.0)
    a_rk = jnp.where(s_i <= t_i, p_k[C:], 0.0)

    nk = a_ab
    tinv = jnp.where(s_i == t_i, 1.0, 0.0) + nk
    n_iter = max(int(np.ceil(np.log2(C))) - 1, 0)
    for _ in range(n_iter):
        nk = _mm(nk, bd_s(nk), _NN, passes)
        tinv = tinv + _mm(nk, bd_s(tinv), _NN, passes)

    sp = []
    for q in range(nsub):
        sp.append(_mm(lhs2[:, 256 * q:256 * (q + 1)], sbd_ref[q], _NT, passes))
    sprod = sp[0] if nsub == 1 else jnp.concatenate(sp, axis=1)
    bdv = bd_h(v)
    w_rhs = sprod[:C] + _mm(a_ak, bdv, _NN, passes)
    u = _mm(tinv, bd_h(w_rhs), _NN, passes)
    o = sprod[C:] + _mm(a_rb, bd_h(u), _NN, passes) + _mm(a_rk, bdv, _NN, passes)

    decay_tot = jnp.exp(Ltot)
    for q in range(nsub):
        sl = slice(256 * q, 256 * (q + 1))
        upd = _mm(jnp.concatenate([u[:, sl], v[:, sl]], axis=0),
                  jnp.concatenate([bh[:, sl], kh[:, sl]], axis=0), _TN, passes)
        sbd_ref[q] = sbd_ref[q] * decay_tot[:, sl] + jnp.where(bd64, upd, 0.0)

    inv_n = 1.0 / HEAD_A
    mean = segsum(o) * inv_n
    d = o - mean
    var = segsum(d * d) * inv_n
    o_n = d * lax.rsqrt(var + GN_EPS) * lnw_ref[...] + lnb_ref[...]
    bonus = segsum(r * kmod * rk_ref[...]) * v
    y_ref[...] = ((o_n + bonus) * g).astype(y_ref.dtype)

    @pl.when(c_idx == n_chunks - 1)
    def _():
        for q in range(nsub):
            rows = _mm_exact_rhs(sbd_ref[q], tile_e, _NT)
            st_ref[0, 4 * q:4 * q + 4] = rows.reshape(4, HEAD_A, HEAD_A)


def wkv_call(r, k, v, lw, a, g, k_k, k_a, r_k, ln_w, ln_b, s0, *, T, C, passes=3):
    M, D = r.shape
    B = M // T
    H = D // HEAD_A
    G = 256 // C
    assert G * C == 256 and G % 4 == 0 and H % G == 0 and T % C == 0
    Lw = G * HEAD_A
    nc = T // C
    row = pl.BlockSpec((C, Lw), lambda bi, hg, c: (bi * nc + c, hg))
    vec = pl.BlockSpec((1, Lw), lambda bi, hg, c: (0, hg))
    st = pl.BlockSpec((1, G, HEAD_A, HEAD_A), lambda bi, hg, c: (bi, hg, 0, 0))
    vecs = [x.reshape(1, D) for x in (k_k, k_a, r_k, ln_w, ln_b)]
    return pl.pallas_call(
        functools.partial(_wkv_kernel, C=C, passes=passes),
        grid=(B, H // G, nc),
        in_specs=[row] * 6 + [vec] * 5 + [st],
        out_specs=[row, st],
        out_shape=[jax.ShapeDtypeStruct((M, D), BF16), jax.ShapeDtypeStruct((B, H, HEAD_A, HEAD_A), F32)],
        scratch_shapes=[pltpu.VMEM((Lw // 256, 256, 256), F32)],
        compiler_params=_cparams(("parallel", "parallel", "arbitrary")),
    )(r, k, v, lw, a, g, *vecs, s0)


def _rope_tables(pos0, T):
    half = 32
    inv = ROPE_THETA ** (-jnp.arange(half, dtype=F32) / half)
    ang = (pos0 + jnp.arange(T)).astype(F32)[:, None] * inv[None, :]
    cos, sin = jnp.cos(ang), jnp.sin(ang)
    return jnp.tile(cos, (1, LANES // half)), jnp.tile(sin, (1, LANES // half))


def _rot_cols(w):
    return jnp.concatenate([-w[..., 32:], w[..., :32]], axis=-1)


def _kv_kernel(a_ref, w_ref, g_ref, cos_ref, sin_ref, ckv_ref, kpe_ref, ckvb_ref, kpeb_ref, *, C_KV):
    kv = jnp.dot(a_ref[...], w_ref[...], preferred_element_type=F32)
    ckv = _rms(kv[:, :C_KV], g_ref[...])
    tail = kv[:, C_KV:C_KV + LANES]
    rot = pltpu.roll(tail, 64, 1)
    lane = _iota2(tail.shape, 1)
    kpe = jnp.where(lane < 64, tail * cos_ref[...] + rot * sin_ref[...], 0.0)
    ckv_ref[...] = ckv
    kpe_ref[...] = kpe[:, :64]
    ckvb_ref[...] = ckv.astype(BF16)
    kpeb_ref[...] = kpe.astype(BF16)


def kv_call(hkv, w_ext, g_a, cos, sin):
    M, D = hkv.shape
    C_KV = g_a.shape[0]
    NW = w_ext.shape[1]
    tm = _pick(M, (512, 256, 128, 64, 32, 16, 8))
    rowb = lambda n: pl.BlockSpec((tm, n), lambda i: (i, 0))
    return pl.pallas_call(
        functools.partial(_kv_kernel, C_KV=C_KV), grid=(M // tm,),
        in_specs=[rowb(D), pl.BlockSpec((D, NW), lambda i: (0, 0)), pl.BlockSpec((1, C_KV), lambda i: (0, 0)),
                  rowb(LANES), rowb(LANES)],
        out_specs=[rowb(C_KV), rowb(64), rowb(C_KV), rowb(LANES)],
        out_shape=[jax.ShapeDtypeStruct((M, C_KV), F32), jax.ShapeDtypeStruct((M, 64), F32),
                   jax.ShapeDtypeStruct((M, C_KV), BF16), jax.ShapeDtypeStruct((M, LANES), BF16)],
        compiler_params=_cparams(("parallel",)),
    )(hkv, w_ext, g_a.reshape(1, C_KV), cos, sin)


def _ep_rmsnorm(accs, e_refs, o_refs):
    o_refs[0][...] = _rms(accs[0], e_refs[0][...]).astype(o_refs[0].dtype)


def _ep_rope_q(accs, e_refs, o_refs):
    acc = accs[0]
    hi = acc[:, LANES:]
    rot = pltpu.roll(hi, 64, 1)
    lane = _iota2(hi.shape, 1)
    pe = jnp.where(lane < 64, hi * e_refs[0][...] + rot * e_refs[1][...], 0.0)
    o_refs[0][...] = jnp.concatenate([acc[:, :LANES], pe], axis=1).astype(o_refs[0].dtype)


def _ep_expand_kv(accs, e_refs, o_refs):
    o_refs[0][...] = jnp.concatenate([accs[0].astype(BF16), e_refs[0][...]], axis=1)
    o_refs[1][...] = accs[1].astype(BF16)


def _attn_prompt_kernel(q_ref, k_ref, v_ref, o_ref, *, tq, scale):
    T = q_ref.shape[0]
    for qi in range(T // tq):
        n = (qi + 1) * tq
        q = q_ref[qi * tq:(qi + 1) * tq, :]
        s = _dg(q, k_ref[:n, :], _NT) * scale
        kpos = _iota2((tq, n), 1)
        qpos = qi * tq + _iota2((tq, n), 0)
        s = jnp.where(kpos <= qpos, s, NEG_INF)
        m = jnp.max(s, axis=-1, keepdims=True)
        p = jnp.exp(s - m)
        l = jnp.sum(p, axis=-1, keepdims=True)
        o = jnp.dot(p.astype(BF16), v_ref[:n, :], preferred_element_type=F32)
        o_ref[qi * tq:(qi + 1) * tq, :] = (o / l).astype(o_ref.dtype)


def attn_prompt_call(q, k, v, *, B, T, H, scale):
    tq = _pick(T, (512, 256, 128, 64, 32, 16, 8))
    return pl.pallas_call(
        functools.partial(_attn_prompt_kernel, tq=tq, scale=scale),
        grid=(B, H),
        in_specs=[pl.BlockSpec((T, 256), lambda b, h: (b, h)), pl.BlockSpec((T, 256), lambda b, h: (b, h)),
                  pl.BlockSpec((T, LANES), lambda b, h: (b, h))],
        out_specs=pl.BlockSpec((T, LANES), lambda b, h: (b, h)),
        out_shape=jax.ShapeDtypeStruct((B * T, H * LANES), BF16),
        compiler_params=_cparams(("parallel", "parallel")),
    )(q, k, v)


def _absorb_q_kernel(q_ref, wk_ref, o_ref):
    q = q_ref[...]
    lat = jnp.dot(q[:, :LANES], wk_ref[0], preferred_element_type=F32)
    o_ref[...] = jnp.concatenate([lat.astype(BF16), q[:, LANES:]], axis=1)


def absorb_q_call(q, wk_t, *, H):
    M = q.shape[0]
    C_KV = wk_t.shape[2]
    return pl.pallas_call(
        _absorb_q_kernel, grid=(H,),
        in_specs=[pl.BlockSpec((M, 256), lambda h: (0, h)), pl.BlockSpec((1, LANES, C_KV), lambda h: (h, 0, 0))],
        out_specs=pl.BlockSpec((M, C_KV + LANES), lambda h: (0, h)),
        out_shape=jax.ShapeDtypeStruct((M, H * (C_KV + LANES)), BF16),
        compiler_params=_cparams(("parallel",)),
    )(q, wk_t)


def _paged_attn_kernel(pt_ref, q_ref, cn_ref, kn_ref, *refs, P, C_KV, Tn, H, scale):
    ckv_refs, kpe_refs = refs[:P], refs[P:2 * P]
    o_ref = refs[2 * P]
    m_ref, l_ref, acc_ref = refs[2 * P + 1:]
    j = pl.program_id(1)
    nj = pl.num_programs(1)
    R = q_ref.shape[1]

    @pl.when(j == 0)
    def _():
        m_ref[...] = jnp.full(m_ref.shape, NEG_INF, F32)
        l_ref[...] = jnp.zeros(l_ref.shape, F32)
        acc_ref[...] = jnp.zeros(acc_ref.shape, F32)

    q = q_ref[0]
    q_lat, q_pe = q[:, :C_KV], q[:, C_KV:C_KV + 64]

    def update(s, vals):
        m_old = m_ref[...]
        m_new = jnp.maximum(m_old, jnp.max(s, axis=-1, keepdims=True))
        corr = jnp.exp(m_old - m_new)
        p = jnp.exp(s - m_new)
        l_ref[...] = l_ref[...] * corr + jnp.sum(p, axis=-1, keepdims=True)
        acc_ref[...] = acc_ref[...] * corr + jnp.dot(p.astype(BF16), vals, preferred_element_type=F32)
        m_ref[...] = m_new

    kc = jnp.concatenate([r[0].astype(BF16) for r in ckv_refs], axis=0)
    kr = jnp.concatenate([r[0].astype(BF16) for r in kpe_refs], axis=0)
    s = (_dg(q_lat, kc, _NT) + _dg(q_pe, kr, _NT)) * scale
    update(s, kc)

    @pl.when(j == nj - 1)
    def _():
        cn = cn_ref[0]
        kn = kn_ref[0][:, :64]
        sn = (_dg(q_lat, cn, _NT) + _dg(q_pe, kn, _NT)) * scale
        t_q = _iota2(sn.shape, 0) // H
        t_k = _iota2(sn.shape, 1)
        sn = jnp.where(t_k <= t_q, sn, NEG_INF)
        update(sn, cn)
        o_ref[0] = (acc_ref[...] / l_ref[...]).astype(o_ref.dtype)


def paged_attn_call(qcat, ckv_new, kpe_new, cache_ckv, cache_kpe, page_table, *, H, Tn, scale):
    Bd, R, QW = qcat.shape
    C_KV = QW - LANES
    n_pages = page_table.shape[1]
    P = _pick(n_pages, (16, 8, 4, 2, 1))
    page = cache_ckv.shape[1]

    def cache_spec(width, i):
        return pl.BlockSpec((1, page, width), lambda b, j, pt: (pt[b, j * P + i], 0, 0))

    in_specs = [pl.BlockSpec((1, R, QW), lambda b, j, pt: (b, 0, 0)),
                pl.BlockSpec((1, 8, C_KV), lambda b, j, pt: (b, 0, 0)),
                pl.BlockSpec((1, 8, LANES), lambda b, j, pt: (b, 0, 0))]
    in_specs += [cache_spec(C_KV, i) for i in range(P)] + [cache_spec(cache_kpe.shape[2], i) for i in range(P)]
    grid_spec = pltpu.PrefetchScalarGridSpec(
        num_scalar_prefetch=1, grid=(Bd, n_pages // P), in_specs=in_specs,
        out_specs=pl.BlockSpec((1, R, C_KV), lambda b, j, pt: (b, 0, 0)),
        scratch_shapes=[pltpu.VMEM((R, 1), F32), pltpu.VMEM((R, 1), F32), pltpu.VMEM((R, C_KV), F32)])
    return pl.pallas_call(
        functools.partial(_paged_attn_kernel, P=P, C_KV=C_KV, Tn=Tn, H=H, scale=scale),
        grid_spec=grid_spec, out_shape=jax.ShapeDtypeStruct((Bd, R, C_KV), BF16),
        compiler_params=_cparams(("parallel", "arbitrary")),
    )(page_table, qcat, ckv_new, kpe_new, *([cache_ckv] * P), *([cache_kpe] * P))


def _unabsorb_kernel(o_ref, wv_ref, out_ref):
    out_ref[...] = jnp.dot(o_ref[...], wv_ref[0], preferred_element_type=F32).astype(out_ref.dtype)


def unabsorb_call(o_lat, wv, *, H):
    M = o_lat.shape[0]
    C_KV = wv.shape[1]
    return pl.pallas_call(
        _unabsorb_kernel, grid=(H,),
        in_specs=[pl.BlockSpec((M, C_KV), lambda h: (0, h)), pl.BlockSpec((1, C_KV, LANES), lambda h: (h, 0, 0))],
        out_specs=pl.BlockSpec((M, LANES), lambda h: (0, h)),
        out_shape=jax.ShapeDtypeStruct((M, H * LANES), BF16),
        compiler_params=_cparams(("parallel",)),
    )(o_lat, wv)


def _pad_cols(w, n):
    return w if w.shape[1] == n else jnp.pad(w, ((0, 0), (0, n - w.shape[1])))


def _pad_rows(w, n):
    return w if w.shape[0] == n else jnp.pad(w, ((0, n - w.shape[0]), (0, 0)))


def _round_up(n, m):
    return -(-n // m) * m


def _prep_weights(p):
    w = {}
    for name in ("rw_w_r", "rw_w_k", "rw_w_v", "rw_w_o", "ffn_w_gate", "ffn_w_up", "ffn_w_down", "q_w_a", "mla_w_o"):
        w[name] = p[name].astype(BF16)
    for n1, n2 in (("rw_w1", "rw_w2"), ("rw_a1", "rw_a2"), ("rw_g1", "rw_g2")):
        R = _round_up(p[n1].shape[2], LANES)
        w[n1] = jnp.stack([_pad_cols(x, R) for x in p[n1]]).astype(BF16)
        w[n2] = jnp.stack([_pad_rows(x, R) for x in p[n2]]).astype(BF16)
    C_KV = p["kv_g_a"].shape[0]
    kv_w_a = p["kv_w_a"]
    w["kv_w_a"] = jnp.concatenate([kv_w_a, _rot_cols(kv_w_a[:, C_KV:])], axis=1).astype(BF16)
    H = p["kv_w_b"].shape[1]
    nope = p["kv_w_b"].shape[2] - LANES
    assert nope == LANES
    Bn, QL, _ = p["q_w_b"].shape
    qb = p["q_w_b"].reshape(Bn, QL, H, nope + 64)
    qb = jnp.concatenate([qb, _rot_cols(qb[..., nope:])], axis=-1)
    w["q_w_b"] = qb.reshape(Bn, QL, H * 256).astype(BF16)
    wk = p["kv_w_b"][..., :nope]
    wv = p["kv_w_b"][..., nope:]
    w["wk_cols"] = wk.reshape(C_KV, H * LANES).astype(BF16)
    w["wv_cols"] = wv.reshape(C_KV, H * LANES).astype(BF16)
    w["wk_t"] = jnp.transpose(wk, (1, 2, 0)).astype(BF16)
    w["wv_h"] = jnp.transpose(wv, (1, 0, 2)).astype(BF16)
    return w


def _ffn(x, mix, l, p, w, tm):
    x, h = norm_call(x, mix, p["g_mix_post"][l], [p["g_ffn_pre"][l]], [BF16])
    M = x.shape[0]
    F = w["ffn_w_gate"].shape[2]
    act = matmul_call(h, [w["ffn_w_gate"][l], w["ffn_w_up"][l]], tm=tm, tn=256, epilogue=_ep_swiglu,
                      outs=[((M, F), BF16, (min(tm, M), 256), lambda i, j: (i, j))])[0]
    y = matmul_call(act, [w["ffn_w_down"][l]], tm=min(tm, 512), tn=256)[0]
    return x, y


def _rwkv_layer(x, shift_prev, s0, p, w, *, B, T, tm, C):
    l = 0
    M, D = x.shape
    (h,) = norm_call(x, pre_gains=[p["g_mix_pre"][l]], pre_dtypes=[F32])
    h3 = h.reshape(B, T, D)
    hp = jnp.concatenate([shift_prev[:, None, :], h3[:, :-1]], axis=1).reshape(M, D)
    shift_out = h3[:, -1]
    m_r, m_w, m_k, m_v, m_a, m_g = mix_call(h, hp, p["rw_mu"][l])
    r = matmul_call(m_r, [w["rw_w_r"][l]], tm=tm, tn=512)[0]
    k = matmul_call(m_k, [w["rw_w_k"][l]], tm=tm, tn=512)[0]
    v = matmul_call(m_v, [w["rw_w_v"][l]], tm=tm, tn=512)[0]
    lw = lora_call(m_w, w["rw_w1"][l], w["rw_w2"][l], p["rw_w0"][l], "decay")
    a = lora_call(m_a, w["rw_a1"][l], w["rw_a2"][l], p["rw_a0"][l], "iclr")
    g = lora_call(m_g, w["rw_g1"][l], w["rw_g2"][l], jnp.zeros((D,), F32), "gate")
    Tp = _round_up(T, C)
    if Tp != T:
        padt = lambda z: jnp.pad(z.reshape(B, T, D), ((0, 0), (0, Tp - T), (0, 0))).reshape(B * Tp, D)
        r, k, v, lw, a, g = [padt(z) for z in (r, k, v, lw, a, g)]
    yg, s_new = wkv_call(r, k, v, lw, a, g, p["rw_k_k"][l], p["rw_k_a"][l], p["rw_r_k"][l].reshape(D),
                         p["rw_ln_w"][l], p["rw_ln_b"][l], s0, T=Tp, C=C)
    if Tp != T:
        yg = yg.reshape(B, Tp, D)[:, :T].reshape(M, D)
    mix = matmul_call(yg, [w["rw_w_o"][l]], tm=tm, tn=512)[0]
    return mix, s_new, shift_out


def _mla_common(x_h1, hkv, p, w, j, *, pos0, B, T, tm):
    M = x_h1.shape[0]
    cos, sin = _rope_tables(pos0, T)
    cos = jnp.tile(cos, (B, 1))
    sin = jnp.tile(sin, (B, 1))
    ckv, kpe, ckv_b, kpe_b = kv_call(hkv, w["kv_w_a"], p["kv_g_a"], cos, sin)
    QL = w["q_w_a"].shape[2]
    qa = matmul_call(x_h1, [w["q_w_a"][j]], tm=min(tm, 512), tn=QL, epilogue=_ep_rmsnorm,
                     extras=[(p["q_g_a"][j].reshape(1, QL), (1, QL), lambda i, jj: (0, 0))],
                     outs=[((M, QL), BF16, (min(tm, 512, M), QL), lambda i, jj: (i, 0))])[0]
    HN = w["q_w_b"].shape[2]
    tmq = min(tm, M)
    q = matmul_call(qa, [w["q_w_b"][j]], tm=tmq, tn=256, epilogue=_ep_rope_q,
                    extras=[(cos, (tmq, LANES), lambda i, jj: (i, 0)), (sin, (tmq, LANES), lambda i, jj: (i, 0))],
                    outs=[((M, HN), BF16, (tmq, 256), lambda i, jj: (i, jj))])[0]
    return ckv, kpe, ckv_b, kpe_b, q


def kernel(x_prompt, x_sample, state_wkv, state_shift, cache_ckv, cache_kpe, page_table, g_mix_pre, g_mix_post,
           g_ffn_pre, g_ffn_post, rw_mu, rw_w_r, rw_w_k, rw_w_v, rw_w_o, rw_w0, rw_w1, rw_w2, rw_a0, rw_a1, rw_a2,
           rw_g1, rw_g2, rw_k_k, rw_k_a, rw_r_k, rw_ln_w, rw_ln_b, kv_g_in, kv_w_a, kv_g_a, kv_w_b, q_w_a, q_g_a,
           q_w_b, mla_w_o, ffn_w_gate, ffn_w_up, ffn_w_down):
    p = dict(g_mix_pre=g_mix_pre, g_mix_post=g_mix_post, g_ffn_pre=g_ffn_pre, g_ffn_post=g_ffn_post,
             rw_mu=rw_mu, rw_w_r=rw_w_r, rw_w_k=rw_w_k, rw_w_v=rw_w_v, rw_w_o=rw_w_o,
             rw_w0=rw_w0, rw_w1=rw_w1, rw_w2=rw_w2, rw_a0=rw_a0, rw_a1=rw_a1, rw_a2=rw_a2,
             rw_g1=rw_g1, rw_g2=rw_g2, rw_k_k=rw_k_k, rw_k_a=rw_k_a, rw_r_k=rw_r_k,
             rw_ln_w=rw_ln_w, rw_ln_b=rw_ln_b, kv_g_in=kv_g_in, kv_w_a=kv_w_a, kv_g_a=kv_g_a, kv_w_b=kv_w_b,
             q_w_a=q_w_a, q_g_a=q_g_a, q_w_b=q_w_b, mla_w_o=mla_w_o,
             ffn_w_gate=ffn_w_gate, ffn_w_up=ffn_w_up, ffn_w_down=ffn_w_down)
    assert g_mix_pre.shape[0] == 2 and rw_mu.shape[0] == 1 and q_w_a.shape[0] == 1, "one RWKV-7 + one MLA layer"
    w = _prep_weights(p)
    D = x_prompt.shape[-1]
    H = kv_w_b.shape[1]
    C_KV = kv_g_a.shape[0]
    scale = float((LANES + 64) ** -0.5)
    outs = {}
    for grp in ("prompt", "sample"):
        if grp == "prompt":
            x3 = x_prompt
            B, T, _ = x3.shape
            s0 = jnp.zeros((B, D // HEAD_A, HEAD_A, HEAD_A), F32)
            shift0 = jnp.zeros((B, D), F32)
            pos0 = 0
            C = 64 if T % 64 == 0 else 8
        else:
            x3 = x_sample
            B, T, _ = x3.shape
            s0 = state_wkv[0]
            shift0 = state_shift[0]
            pos0 = page_table.shape[1] * PAGE_SIZE
            C = 8
        M = B * T
        tm = _pick(M, (1024, 512, 256, 128, 64, 32, 16, 8))
        x = x3.reshape(M, D)
        mix, s_new, shift_out = _rwkv_layer(x, shift0, s0, p, w, B=B, T=T, tm=tm, C=C)
        x, y = _ffn(x, mix, 0, p, w, tm)
        x, h1, hkv = norm_call(x, y, g_ffn_post[0], [g_mix_pre[1], kv_g_in], [BF16, BF16])
        ckv, kpe, ckv_b, kpe_b, q = _mla_common(h1, hkv, p, w, 0, pos0=pos0, B=B, T=T, tm=tm)
        if grp == "prompt":
            tme = min(tm, M)
            k_all, v_all = matmul_call(
                ckv_b, [w["wk_cols"], w["wv_cols"]], tm=tme, tn=LANES, epilogue=_ep_expand_kv,
                extras=[(kpe_b, (tme, LANES), lambda i, jj: (i, 0))],
                outs=[((M, H * 256), BF16, (tme, 256), lambda i, jj: (i, jj)),
                      ((M, H * LANES), BF16, (tme, LANES), lambda i, jj: (i, jj))])
            o = attn_prompt_call(q, k_all, v_all, B=B, T=T, H=H, scale=scale)
        else:
            qcat = absorb_q_call(q, w["wk_t"], H=H).reshape(B, T * H, C_KV + LANES)
            pad8 = lambda z: jnp.pad(z.reshape(B, T, z.shape[-1]), ((0, 0), (0, 8 - T), (0, 0)))
            o_lat = paged_attn_call(qcat, pad8(ckv_b), pad8(kpe_b), cache_ckv, cache_kpe, page_table,
                                    H=H, Tn=T, scale=scale)
            o = unabsorb_call(o_lat.reshape(M, H * C_KV), w["wv_h"], H=H)
        mix = matmul_call(o, [w["mla_w_o"][0]], tm=min(tm, 512), tn=512)[0]
        x, y = _ffn(x, mix, 1, p, w, tm)
        (x,) = norm_call(x, y, g_ffn_post[1])
        outs[grp] = (x.reshape(B, T, D), s_new[None], shift_out[None], ckv.reshape(B, T, C_KV),
                     kpe.reshape(B, T, 64))
    yp, wp, sp, cp, kp = outs["prompt"]
    ys, ws, ss, cs, ks = outs["sample"]
    return (yp, ys, wp, sp, cp, kp, ws, ss, cs, ks)
```

```python
import functools

import jax
import jax.numpy as jnp
from jax import lax
from jax.experimental import pallas as pl
from jax.experimental.pallas import tpu as pltpu

F32 = jnp.float32
BF16 = jnp.bfloat16

V7X_VMEM_BYTES = 64 * 1024 * 1024
LANES = 128
VMEM_LIMIT = V7X_VMEM_BYTES - 8 * 1024 * 1024

HEAD_A = 64
WKV_CHUNK = 64
WKV_GROUP = 256
GN_EPS = 64e-5
RMS_EPS = 1e-6
ROPE_THETA = 10000.0
ROPE_DIM = 64
PAGE_SIZE = 128
NEG_INF = float("-inf")


def _cparams(sem):
    return pltpu.CompilerParams(dimension_semantics=sem, vmem_limit_bytes=VMEM_LIMIT)


def _pick(n, cands):
    for c in cands:
        if n % c == 0:
            return c
    return n


def _rms(x, g):
    return x * lax.rsqrt(jnp.mean(x * x, axis=-1, keepdims=True) + RMS_EPS) * g


_NN = (((1,), (0,)), ((), ()))
_NT = (((1,), (1,)), ((), ()))
_TN = (((0,), (0,)), ((), ()))


def _dg(a, b, dims):
    return lax.dot_general(a, b, dims, preferred_element_type=F32)


def _iota2(shape, dim):
    return lax.broadcasted_iota(jnp.int32, shape, dim)


def _norm_kernel(*refs, has_add, n_pre, pre_dtypes):
    it = iter(refs)
    x_ref = next(it)
    if has_add:
        mix_ref = next(it)
        gpost_ref = next(it)
    gpre_refs = [next(it) for _ in range(n_pre)]
    x = x_ref[...]
    if has_add:
        x = x + _rms(mix_ref[...], gpost_ref[...])
        xo_ref = next(it)
        xo_ref[...] = x
    if n_pre:
        xn = x * lax.rsqrt(jnp.mean(x * x, axis=-1, keepdims=True) + RMS_EPS)
        for g_ref, dt in zip(gpre_refs, pre_dtypes):
            o_ref = next(it)
            o_ref[...] = (xn * g_ref[...]).astype(dt)


def norm_call(x, mix=None, g_post=None, pre_gains=(), pre_dtypes=()):
    M, D = x.shape
    tr = _pick(M, (256, 128, 64, 32, 16, 8))
    row = pl.BlockSpec((tr, D), lambda i: (i, 0))
    vec = pl.BlockSpec((1, D), lambda i: (0, 0))
    has_add = mix is not None
    ins, in_specs = [x], [row]
    if has_add:
        ins += [mix, g_post.reshape(1, D)]
        in_specs += [row, vec]
    for g in pre_gains:
        ins.append(g.reshape(1, D))
        in_specs.append(vec)
    out_shape, out_specs = [], []
    if has_add:
        out_shape.append(jax.ShapeDtypeStruct((M, D), F32))
        out_specs.append(row)
    for dt in pre_dtypes:
        out_shape.append(jax.ShapeDtypeStruct((M, D), dt))
        out_specs.append(row)
    return pl.pallas_call(
        functools.partial(_norm_kernel, has_add=has_add, n_pre=len(pre_gains), pre_dtypes=tuple(pre_dtypes)),
        grid=(M // tr,), in_specs=in_specs, out_specs=out_specs, out_shape=out_shape,
        compiler_params=_cparams(("parallel",)), name="rmsnorm",
    )(*ins)


def _mix_kernel(h_ref, hp_ref, mu_ref, *o_refs):
    h = h_ref[...]
    dx = hp_ref[...] - h
    for j, o_ref in enumerate(o_refs):
        o_ref[...] = (h + dx * mu_ref[j:j + 1, :]).astype(BF16)


def mix_call(h, hp, mu):
    M, D = h.shape
    n = mu.shape[0]
    tr = _pick(M, (256, 128, 64, 32, 16, 8))
    row = pl.BlockSpec((tr, D), lambda i: (i, 0))
    return pl.pallas_call(
        _mix_kernel, grid=(M // tr,),
        in_specs=[row, row, pl.BlockSpec((n, D), lambda i: (0, 0))],
        out_specs=[row] * n, out_shape=[jax.ShapeDtypeStruct((M, D), BF16)] * n,
        compiler_params=_cparams(("parallel",)), name="token_shift_mix",
    )(h, hp, mu)


def _matmul_kernel(a_ref, *refs, nb, ne, epilogue):
    b_refs, e_refs, o_refs = refs[:nb], refs[nb:nb + ne], refs[nb + ne:]
    a = a_ref[...]
    accs = [jnp.dot(a, b[...], preferred_element_type=F32) for b in b_refs]
    epilogue(accs, e_refs, o_refs)


def _ep_store(accs, e_refs, o_refs):
    o_refs[0][...] = accs[0].astype(o_refs[0].dtype)


def _ep_swiglu(accs, e_refs, o_refs):
    g, u = accs
    o_refs[0][...] = (g * jax.nn.sigmoid(g) * u).astype(o_refs[0].dtype)


def matmul_call(a, bs, *, tm, tn, name, epilogue=_ep_store, extras=(), outs=None):
    M, K = a.shape
    N = bs[0].shape[1]
    tm, tn = min(tm, M), min(tn, N)
    assert M % tm == 0 and N % tn == 0, (M, N, tm, tn)
    if outs is None:
        outs = [((M, N), F32, (tm, tn), lambda i, j: (i, j))]
    in_specs = [pl.BlockSpec((tm, K), lambda i, j: (i, 0))]
    in_specs += [pl.BlockSpec((K, tn), lambda i, j: (0, j)) for _ in bs]
    in_specs += [pl.BlockSpec(bshape, imap) for _, bshape, imap in extras]
    return pl.pallas_call(
        functools.partial(_matmul_kernel, nb=len(bs), ne=len(extras), epilogue=epilogue),
        grid=(M // tm, N // tn), in_specs=in_specs,
        out_specs=[pl.BlockSpec(bshape, imap) for _, _, bshape, imap in outs],
        out_shape=[jax.ShapeDtypeStruct(shape, dt) for shape, dt, _, _ in outs],
        compiler_params=_cparams(("parallel", "arbitrary")), name=name,
    )(a, *bs, *[e[0] for e in extras])


def _lora_kernel(a_ref, w1_ref, w2_ref, bias_ref, o_ref, *, kind):
    z = jnp.dot(a_ref[...], w1_ref[...], preferred_element_type=F32)
    if kind == "decay":
        z = jnp.tanh(z)
    elif kind == "gate":
        z = jax.nn.sigmoid(z)
    y = jnp.dot(z.astype(BF16), w2_ref[...], preferred_element_type=F32)
    if kind == "decay":
        y = -jnp.exp(-jax.nn.softplus(-(bias_ref[...] + y)) - 0.5)
    elif kind == "iclr":
        y = jax.nn.sigmoid(bias_ref[...] + y)
    o_ref[...] = y


def lora_call(a, w1, w2, bias, kind):
    M, D = a.shape
    R = w1.shape[1]
    tm = _pick(M, (512, 256, 128, 64, 32, 16, 8))
    return pl.pallas_call(
        functools.partial(_lora_kernel, kind=kind), grid=(M // tm,),
        in_specs=[pl.BlockSpec((tm, D), lambda i: (i, 0)), pl.BlockSpec((D, R), lambda i: (0, 0)),
                  pl.BlockSpec((R, D), lambda i: (0, 0)), pl.BlockSpec((1, D), lambda i: (0, 0))],
        out_specs=pl.BlockSpec((tm, D), lambda i: (i, 0)),
        out_shape=jax.ShapeDtypeStruct((M, D), F32),
        compiler_params=_cparams(("parallel",)), name="lora_" + kind,
    )(a, w1, w2, bias.reshape(1, D))


def _split2(x):
    hi = x.astype(BF16)
    lo = (x - hi.astype(F32)).astype(BF16)
    return hi, lo


def _split3(x):
    h1 = x.astype(BF16)
    r1 = x - h1.astype(F32)
    h2 = r1.astype(BF16)
    h3 = (r1 - h2.astype(F32)).astype(BF16)
    return h1, h2, h3


def _mm_exact3(a, b_bf16, dims):
    h1, h2, h3 = _split3(a)
    return _dg(h1, b_bf16, dims) + (_dg(h2, b_bf16, dims) + _dg(h3, b_bf16, dims))


def _mm_exact2(a, b_bf16, dims):
    hi, lo = _split2(a)
    return _dg(hi, b_bf16, dims) + _dg(lo, b_bf16, dims)


def _wkv_groups(r, k, v, lw, a, g, kkp, kap, rkp, lnw, lnb, s_bd, ones_bd, tri):
    C = WKV_CHUNK
    G = WKV_GROUP // HEAD_A

    def each(f, *ls):
        return [f(*xs) for xs in zip(*ls)]

    def bd(x):
        return jnp.concatenate([x.astype(BF16)] * G, axis=0) * ones_bd

    def mm(x, y_bd):
        return _dg(x.astype(BF16), y_bd, _NN)

    def cat(x, y):
        return jnp.concatenate([x, y], axis=0)

    kk = each(lambda k_, p_: k_ * p_, k, kkp)
    ssq = each(lambda x: _mm_exact2(x * x, ones_bd, _NN), kk)
    kk = each(lambda x, s: x * lax.rsqrt(jnp.maximum(s, 1e-24)), kk, ssq)
    kmod = each(lambda k_, a_, p_: k_ * (1.0 + (a_ - 1.0) * p_), k, a, kap)
    b = each(lambda x, a_: x * a_, kk, a)

    def cumsum(x):
        l1, l2, l3 = _split3(x)
        return _dg(tri, l1, _NN) + (_dg(tri, l2, _NN) + _dg(tri, l3, _NN))

    L = each(cumsum, lw)
    Ltot = each(lambda x: x[C - 1:C, :], L)
    enL = each(lambda x: jnp.exp(-x), L)
    eR = each(lambda t_, x: jnp.exp(t_ - x), Ltot, L)
    at = each(lambda x, l_, w_: -x * jnp.exp(l_ - w_), kk, L, lw)
    bt = each(lambda x, e: x * e, b, enL)
    kt = each(lambda x, e: x * e, kmod, enL)
    rt = each(lambda x, l_: x * jnp.exp(l_), r, L)
    bh = each(lambda x, e: x * e, b, eR)
    kh = each(lambda x, e: x * e, kmod, eR)

    t_i = _iota2((C, WKV_GROUP), 0)
    s_i = _iota2((C, WKV_GROUP), 1) % C
    lhs2 = each(lambda x, y: cat(x, y).astype(BF16), at, rt)
    p_b = each(lambda x, y: _dg(x, bd(y), _NT), lhs2, bt)
    p_k = each(lambda x, y: _dg(x, bd(y), _NT), lhs2, kt)
    a_ab = each(lambda x: jnp.where(s_i < t_i, x[:C], 0.0), p_b)
    a_rb = each(lambda x: jnp.where(s_i <= t_i, x[C:], 0.0), p_b)
    a_ak = each(lambda x: jnp.where(s_i < t_i, x[:C], 0.0), p_k)
    a_rk = each(lambda x: jnp.where(s_i <= t_i, x[C:], 0.0), p_k)

    eye = jnp.where(s_i == t_i, 1.0, 0.0)
    tinv = each(lambda x: eye + x, a_ab)
    nk = each(lambda x: mm(x, bd(x)), a_ab)
    for _ in range(4):
        both = each(lambda n_, t_: mm(cat(n_, t_), bd(n_)), nk, tinv)
        tinv = each(lambda t_, x: t_ + x[C:], tinv, both)
        nk = each(lambda x: x[:C], both)
    tinv = each(lambda t_, n_: t_ + mm(t_, bd(n_)), tinv, nk)

    sprod = each(lambda x, s: _dg(x, s.astype(BF16), _NT), lhs2, s_bd)
    akrk = each(lambda x, y, v_: mm(cat(x, y), bd(v_)), a_ak, a_rk, v)
    u = each(lambda t_, s, x: mm(t_, bd(s[:C] + x[:C])), tinv, sprod, akrk)
    o = each(lambda s, x, m_, u_: s[C:] + x[C:] + mm(m_, bd(u_)), sprod, akrk, a_rb, u)

    upd = each(lambda u_, v_, b_, k_: _dg(cat(u_, v_).astype(BF16), cat(b_, k_).astype(BF16), _TN),
               u, v, bh, kh)
    mask_f = ones_bd.astype(F32)
    s_new = each(lambda s, t_, x: s * jnp.exp(t_) + x * mask_f, s_bd, Ltot, upd)

    inv_n = 1.0 / HEAD_A
    sums = each(lambda o_, r_, k_, p_: _mm_exact2(cat(o_, r_ * k_ * p_), ones_bd, _NN), o, r, kmod, rkp)
    d = each(lambda o_, s: o_ - s[:C] * inv_n, o, sums)
    var = each(lambda x: _mm_exact2(x * x, ones_bd, _NN) * inv_n, d)
    y = each(lambda d_, var_, w_, b_, s, v_, g_: (d_ * lax.rsqrt(var_ + GN_EPS) * w_ + b_ + s[C:] * v_) * g_,
             d, var, lnw, lnb, sums, v, g)
    return y, s_new


def _wkv_chunk_kernel(r_ref, k_ref, v_ref, lw_ref, a_ref, g_ref, kk_ref, ka_ref, rk_ref, lnw_ref, lnb_ref,
                      y_ref, st_ref, sbd_ref):
    C = WKV_CHUNK
    ng = r_ref.shape[1] // WKV_GROUP
    c_idx = pl.program_id(2)
    W = WKV_GROUP
    same_head = (_iota2((W, W), 0) // HEAD_A) == (_iota2((W, W), 1) // HEAD_A)
    ones_bd = jnp.where(same_head, 1.0, 0.0).astype(BF16)
    tri = jnp.where(_iota2((C, C), 1) <= _iota2((C, C), 0), 1.0, 0.0).astype(BF16)

    @pl.when(c_idx == 0)
    def _():
        sbd_ref[...] = jnp.zeros(sbd_ref.shape, F32)

    sls = [slice(W * q, W * (q + 1)) for q in range(ng)]
    split = lambda ref: [ref[:, sl] for sl in sls]
    ys, s_news = _wkv_groups(*[split(ref) for ref in (r_ref, k_ref, v_ref, lw_ref, a_ref, g_ref, kk_ref, ka_ref,
                                                      rk_ref, lnw_ref, lnb_ref)],
                             [sbd_ref[q] for q in range(ng)], ones_bd, tri)
    for q in range(ng):
        y_ref[:, sls[q]] = ys[q].astype(y_ref.dtype)
        sbd_ref[q] = s_news[q]

    @pl.when(c_idx == pl.num_programs(2) - 1)
    def _():
        fold = jnp.where((_iota2((HEAD_A, W), 1) % HEAD_A) == _iota2((HEAD_A, W), 0), 1.0, 0.0).astype(BF16)
        for q in range(ng):
            rows = _mm_exact3(sbd_ref[q], fold, _NT)
            st_ref[0, 4 * q:4 * q + 4] = rows.reshape(4, HEAD_A, HEAD_A)


def wkv_chunk_call(r, k, v, lw, a, g, k_k, k_a, r_k, ln_w, ln_b, *, T):
    M, D = r.shape
    B = M // T
    H = D // HEAD_A
    C = WKV_CHUNK
    assert C == HEAD_A and T % C == 0
    Lw = _pick(D, (4 * WKV_GROUP, 2 * WKV_GROUP, WKV_GROUP))
    assert D % Lw == 0 and Lw % WKV_GROUP == 0
    hb = Lw // HEAD_A
    nc = T // C
    row = pl.BlockSpec((C, Lw), lambda bi, hg, c: (bi * nc + c, hg))
    vec = pl.BlockSpec((1, Lw), lambda bi, hg, c: (0, hg))
    st = pl.BlockSpec((1, hb, HEAD_A, HEAD_A), lambda bi, hg, c: (bi, hg, 0, 0))
    vecs = [x.reshape(1, D) for x in (k_k, k_a, r_k, ln_w, ln_b)]
    return pl.pallas_call(
        _wkv_chunk_kernel,
        grid=(B, D // Lw, nc),
        in_specs=[row] * 6 + [vec] * 5,
        out_specs=[row, st],
        out_shape=[jax.ShapeDtypeStruct((M, D), BF16), jax.ShapeDtypeStruct((B, H, HEAD_A, HEAD_A), F32)],
        scratch_shapes=[pltpu.VMEM((Lw // WKV_GROUP, WKV_GROUP, WKV_GROUP), F32)],
        compiler_params=_cparams(("parallel", "parallel", "arbitrary")), name="wkv_chunk",
    )(r, k, v, lw, a, g, *vecs)


def _wkv_step_kernel(r_ref, k_ref, v_ref, lw_ref, a_ref, g_ref, kk_ref, ka_ref, rk_ref, lnw_ref, lnb_ref,
                     s_ref, y_ref, so_ref, op_ref, sa_ref, o_ref, *, T, hb):
    N = HEAD_A
    for h in range(hb):
        sl = slice(h * N, (h + 1) * N)
        for t in range(T):
            k = k_ref[t, sl, :]
            a = a_ref[t, sl, :]
            kk = k * kk_ref[sl, :]
            kk = kk * lax.rsqrt(jnp.maximum(jnp.sum(kk * kk, axis=0, keepdims=True), 1e-24))
            op_ref[t, 0] = jnp.exp(lw_ref[t, sl, :])
            op_ref[t, 1] = kk * a
            op_ref[t, 2] = k * (1.0 + (a - 1.0) * ka_ref[sl, :])
            op_ref[t, 3] = r_ref[t, sl, :]
            op_ref[t, 4] = -kk

        def first(vi, carry, h=h):
            sa_ref[0, pl.ds(vi, 1), :] = jnp.sum(s_ref[h, vi] * op_ref[0, 4], axis=0, keepdims=True)
            return carry

        lax.fori_loop(0, N, first, 0, unroll=8)
        for t in range(T):
            src = s_ref if t == 0 else so_ref
            cur, nxt = t % 2, (t + 1) % 2

            def body(vi, carry, h=h, t=t, src=src, cur=cur, nxt=nxt):
                s_row = src[h, vi]
                sa_row = sa_ref[cur, pl.ds(vi, 1), :]
                v_row = v_ref[t, pl.ds(h * N + vi, 1), :]
                s_new = s_row * op_ref[t, 0] + sa_row * op_ref[t, 1] + v_row * op_ref[t, 2]
                so_ref[h, vi] = s_new
                o_ref[pl.ds(vi, 1), :] = jnp.sum(s_new * op_ref[t, 3], axis=0, keepdims=True)
                if t + 1 < T:
                    sa_ref[nxt, pl.ds(vi, 1), :] = jnp.sum(s_new * op_ref[t + 1, 4], axis=0, keepdims=True)
                return carry

            lax.fori_loop(0, N, body, 0, unroll=2)
            o = o_ref[...]
            d = o - jnp.mean(o, axis=0, keepdims=True)
            var = jnp.mean(d * d, axis=0, keepdims=True)
            o_n = d * lax.rsqrt(var + GN_EPS) * lnw_ref[sl, :] + lnb_ref[sl, :]
            bonus = jnp.sum(op_ref[t, 3] * op_ref[t, 2] * rk_ref[sl, :], axis=0, keepdims=True) * v_ref[t, sl, :]
            y_ref[t, sl, :] = ((o_n + bonus) * g_ref[t, sl, :]).astype(y_ref.dtype)


def wkv_step_call(r, k, v, lw, a, g, k_k, k_a, r_k, ln_w, ln_b, s0, *, T):
    _, D, Bd = r.shape
    H = D // HEAD_A
    hb = 2 if H % 2 == 0 else 1
    seq = pl.BlockSpec((T, hb * HEAD_A, Bd), lambda i: (0, i, 0))
    par = pl.BlockSpec((hb * HEAD_A, Bd), lambda i: (i, 0))
    st = pl.BlockSpec((hb, HEAD_A, HEAD_A, Bd), lambda i: (i, 0, 0, 0))
    pars = [jnp.broadcast_to(x.reshape(D, 1), (D, Bd)) for x in (k_k, k_a, r_k, ln_w, ln_b)]
    return pl.pallas_call(
        functools.partial(_wkv_step_kernel, T=T, hb=hb), grid=(H // hb,),
        in_specs=[seq] * 6 + [par] * 5 + [st],
        out_specs=[seq, st],
        out_shape=[jax.ShapeDtypeStruct((T, D, Bd), BF16), jax.ShapeDtypeStruct((H, HEAD_A, HEAD_A, Bd), F32)],
        scratch_shapes=[pltpu.VMEM((T, 5, HEAD_A, Bd), F32), pltpu.VMEM((2, HEAD_A, Bd), F32),
                        pltpu.VMEM((HEAD_A, Bd), F32)],
        compiler_params=_cparams(("parallel",)), name="wkv_step",
    )(r, k, v, lw, a, g, *pars, s0)


def _rope_tables(pos0, T):
    half = ROPE_DIM // 2
    inv = ROPE_THETA ** (-jnp.arange(half, dtype=F32) / half)
    ang = (pos0 + jnp.arange(T)).astype(F32)[:, None] * inv[None, :]
    cos, sin = jnp.cos(ang), jnp.sin(ang)
    return jnp.tile(cos, (1, LANES // half)), jnp.tile(sin, (1, LANES // half))


def _rot_cols(w):
    half = ROPE_DIM // 2
    return jnp.concatenate([-w[..., half:], w[..., :half]], axis=-1)


def _rope_tail(tail, cos, sin):
    rot = pltpu.roll(tail, ROPE_DIM, 1)
    return jnp.where(_iota2(tail.shape, 1) < ROPE_DIM, tail * cos + rot * sin, 0.0)


def _kv_kernel(a_ref, w_ref, g_ref, cos_ref, sin_ref, ckv_ref, kpe_ref, ckvb_ref, kpeb_ref, *, C_KV):
    kv = jnp.dot(a_ref[...], w_ref[...], preferred_element_type=F32)
    ckv = _rms(kv[:, :C_KV], g_ref[...])
    kpe = _rope_tail(kv[:, C_KV:C_KV + LANES], cos_ref[...], sin_ref[...])
    ckv_ref[...] = ckv
    kpe_ref[...] = kpe[:, :ROPE_DIM]
    ckvb_ref[...] = ckv.astype(BF16)
    kpeb_ref[...] = kpe.astype(BF16)


def kv_call(hkv, w_ext, g_a, cos, sin):
    M, D = hkv.shape
    C_KV = g_a.shape[0]
    NW = w_ext.shape[1]
    tm = _pick(M, (512, 256, 128, 64, 32, 16, 8))
    rowb = lambda n: pl.BlockSpec((tm, n), lambda i: (i, 0))
    return pl.pallas_call(
        functools.partial(_kv_kernel, C_KV=C_KV), grid=(M // tm,),
        in_specs=[rowb(D), pl.BlockSpec((D, NW), lambda i: (0, 0)), pl.BlockSpec((1, C_KV), lambda i: (0, 0)),
                  rowb(LANES), rowb(LANES)],
        out_specs=[rowb(C_KV), rowb(ROPE_DIM), rowb(C_KV), rowb(LANES)],
        out_shape=[jax.ShapeDtypeStruct((M, C_KV), F32), jax.ShapeDtypeStruct((M, ROPE_DIM), F32),
                   jax.ShapeDtypeStruct((M, C_KV), BF16), jax.ShapeDtypeStruct((M, LANES), BF16)],
        compiler_params=_cparams(("parallel",)), name="latent_kv",
    )(hkv, w_ext, g_a.reshape(1, C_KV), cos, sin)


def _ep_rmsnorm(accs, e_refs, o_refs):
    o_refs[0][...] = _rms(accs[0], e_refs[0][...]).astype(o_refs[0].dtype)


def _ep_rope_q(accs, e_refs, o_refs, *, scale):
    acc = accs[0]
    pe = _rope_tail(acc[:, LANES:], e_refs[0][...], e_refs[1][...])
    o_refs[0][...] = (jnp.concatenate([acc[:, :LANES], pe], axis=1) * scale).astype(o_refs[0].dtype)


def _ep_expand_kv(accs, e_refs, o_refs):
    k2 = accs[0].astype(BF16)
    pe = e_refs[0][...]
    o_refs[0][...] = jnp.concatenate([k2[:, :LANES], pe, k2[:, LANES:], pe], axis=1)
    o_refs[1][...] = accs[1].astype(BF16)


def _attn_prompt_kernel(q_ref, k_ref, v_ref, o_ref, *, tq):
    T = q_ref.shape[0]
    causal = _iota2((tq, tq), 1) <= _iota2((tq, tq), 0)
    for qi in range(T // tq):
        n = (qi + 1) * tq
        q = q_ref[qi * tq:(qi + 1) * tq, :]
        s = _dg(q, k_ref[:n, :], _NT)
        diag = jnp.where(causal, s[:, n - tq:], NEG_INF)
        s = diag if qi == 0 else jnp.concatenate([s[:, :n - tq], diag], axis=1)
        m = jnp.max(s, axis=-1, keepdims=True)
        p = jnp.exp(s - m)
        l = jnp.sum(p, axis=-1, keepdims=True)
        o = jnp.dot(p.astype(BF16), v_ref[:n, :], preferred_element_type=F32)
        o_ref[qi * tq:(qi + 1) * tq, :] = (o / l).astype(o_ref.dtype)


def attn_prompt_call(q, k, v, *, B, T, H):
    tq = _pick(T, (512, 256, 128))
    assert T % tq == 0 and tq % LANES == 0
    return pl.pallas_call(
        functools.partial(_attn_prompt_kernel, tq=tq),
        grid=(B, H),
        in_specs=[pl.BlockSpec((T, 256), lambda b, h: (b, h)), pl.BlockSpec((T, 256), lambda b, h: (b, h)),
                  pl.BlockSpec((T, LANES), lambda b, h: (b, h))],
        out_specs=pl.BlockSpec((T, LANES), lambda b, h: (b, h)),
        out_shape=jax.ShapeDtypeStruct((B * T, H * LANES), BF16),
        compiler_params=_cparams(("parallel", "parallel")), name="attn_prompt",
    )(q, k, v)


def _absorb_q_kernel(q_ref, wk_ref, o_ref):
    q = q_ref[...]
    lat = jnp.dot(q[:, :LANES], wk_ref[0], preferred_element_type=F32)
    o_ref[...] = jnp.concatenate([lat.astype(BF16), q[:, LANES:]], axis=1)


def absorb_q_call(q, wk_t, *, H):
    M = q.shape[0]
    C_KV = wk_t.shape[2]
    return pl.pallas_call(
        _absorb_q_kernel, grid=(H,),
        in_specs=[pl.BlockSpec((M, 256), lambda h: (0, h)), pl.BlockSpec((1, LANES, C_KV), lambda h: (h, 0, 0))],
        out_specs=pl.BlockSpec((M, C_KV + LANES), lambda h: (0, h)),
        out_shape=jax.ShapeDtypeStruct((M, H * (C_KV + LANES)), BF16),
        compiler_params=_cparams(("parallel",)), name="absorb_q",
    )(q, wk_t)


def _paged_attn_kernel(pt_ref, q_ref, cn_ref, kn_ref, *refs, P, C_KV, H, n_chain):
    ckv_refs, kpe_refs = refs[:P], refs[P:2 * P]
    o_ref = refs[2 * P]
    m_ref, l_ref, acc_ref = refs[2 * P + 1:]
    j = pl.program_id(1)

    @pl.when(j == 0)
    def _():
        m_ref[...] = jnp.full(m_ref.shape, NEG_INF, F32)
        l_ref[...] = jnp.zeros(l_ref.shape, F32)
        acc_ref[...] = jnp.zeros(acc_ref.shape, F32)

    q = q_ref[0]
    q_lat, q_pe = q[:, :C_KV], q[:, C_KV:C_KV + ROPE_DIM]

    def update(c, s, vals):
        m_old = m_ref[c]
        m_new = jnp.maximum(m_old, jnp.max(s, axis=-1, keepdims=True))
        corr = jnp.exp(m_old - m_new)
        p = jnp.exp(s - m_new)
        l_ref[c] = l_ref[c] * corr + jnp.sum(p, axis=-1, keepdims=True)
        acc_ref[c] = acc_ref[c] * corr + jnp.dot(p.astype(BF16), vals, preferred_element_type=F32)
        m_ref[c] = m_new

    per = P // n_chain
    kcs, scores = [], []
    for c in range(n_chain):
        kc = jnp.concatenate([r[0].astype(BF16) for r in ckv_refs[c * per:(c + 1) * per]], axis=0)
        kr = jnp.concatenate([r[0].astype(BF16) for r in kpe_refs[c * per:(c + 1) * per]], axis=1)
        kcs.append(kc)
        scores.append(_dg(q_lat, kc, _NT) + _dg(q_pe, kr, _NN))
    for c in range(n_chain):
        update(c, scores[c], kcs[c])

    @pl.when(j == pl.num_programs(1) - 1)
    def _():
        cn = cn_ref[0]
        kn = kn_ref[0][:, :ROPE_DIM]
        sn = _dg(q_lat, cn, _NT) + _dg(q_pe, kn, _NT)
        sn = jnp.where(_iota2(sn.shape, 1) <= _iota2(sn.shape, 0) // H, sn, NEG_INF)
        update(0, sn, cn)
        m = m_ref[0]
        for c in range(1, n_chain):
            m = jnp.maximum(m, m_ref[c])
        acc = jnp.zeros(acc_ref.shape[1:], F32)
        l = jnp.zeros(l_ref.shape[1:], F32)
        for c in range(n_chain):
            wgt = jnp.exp(m_ref[c] - m)
            acc = acc + acc_ref[c] * wgt
            l = l + l_ref[c] * wgt
        o_ref[0] = (acc / l).astype(o_ref.dtype)


def paged_attn_call(qcat, ckv_new, kpe_new, cache_ckv, cache_kpe_t, page_table, *, H):
    Bd, R, QW = qcat.shape
    C_KV = QW - LANES
    n_pages = page_table.shape[1]
    P = _pick(n_pages, (32, 16, 8, 4, 2))
    n_chain = _pick(P, (4, 2, 1))
    page = cache_ckv.shape[1]

    def cache_spec(shape, i):
        return pl.BlockSpec((1,) + shape, lambda b, j, pt: (pt[b, j * P + i], 0, 0))

    in_specs = [pl.BlockSpec((1, R, QW), lambda b, j, pt: (b, 0, 0)),
                pl.BlockSpec((1, 8, C_KV), lambda b, j, pt: (b, 0, 0)),
                pl.BlockSpec((1, 8, LANES), lambda b, j, pt: (b, 0, 0))]
    in_specs += [cache_spec((page, C_KV), i) for i in range(P)] + [cache_spec((ROPE_DIM, page), i) for i in range(P)]
    grid_spec = pltpu.PrefetchScalarGridSpec(
        num_scalar_prefetch=1, grid=(Bd, n_pages // P), in_specs=in_specs,
        out_specs=pl.BlockSpec((1, R, C_KV), lambda b, j, pt: (b, 0, 0)),
        scratch_shapes=[pltpu.VMEM((n_chain, R, 1), F32), pltpu.VMEM((n_chain, R, 1), F32),
                        pltpu.VMEM((n_chain, R, C_KV), F32)])
    return pl.pallas_call(
        functools.partial(_paged_attn_kernel, P=P, C_KV=C_KV, H=H, n_chain=n_chain),
        grid_spec=grid_spec, out_shape=jax.ShapeDtypeStruct((Bd, R, C_KV), BF16),
        compiler_params=_cparams(("parallel", "arbitrary")), name="paged_attn",
    )(page_table, qcat, ckv_new, kpe_new, *([cache_ckv] * P), *([cache_kpe_t] * P))


def _unabsorb_kernel(o_ref, wv_ref, out_ref):
    out_ref[...] = jnp.dot(o_ref[...], wv_ref[0], preferred_element_type=F32).astype(out_ref.dtype)


def unabsorb_call(o_lat, wv, *, H):
    M = o_lat.shape[0]
    C_KV = wv.shape[1]
    return pl.pallas_call(
        _unabsorb_kernel, grid=(H,),
        in_specs=[pl.BlockSpec((M, C_KV), lambda h: (0, h)), pl.BlockSpec((1, C_KV, LANES), lambda h: (h, 0, 0))],
        out_specs=pl.BlockSpec((M, LANES), lambda h: (0, h)),
        out_shape=jax.ShapeDtypeStruct((M, H * LANES), BF16),
        compiler_params=_cparams(("parallel",)), name="unabsorb_o",
    )(o_lat, wv)


def _pad_cols(w, n):
    return w if w.shape[1] == n else jnp.pad(w, ((0, 0), (0, n - w.shape[1])))


def _pad_rows(w, n):
    return w if w.shape[0] == n else jnp.pad(w, ((0, n - w.shape[0]), (0, 0)))


def _round_up(n, m):
    return -(-n // m) * m


def _prep_weights(p):
    w = {}
    for name in ("rw_w_r", "rw_w_k", "rw_w_v", "rw_w_o", "ffn_w_gate", "ffn_w_up", "ffn_w_down", "q_w_a", "mla_w_o"):
        w[name] = [p[name][l].astype(BF16) for l in range(p[name].shape[0])]
    for n1, n2 in (("rw_w1", "rw_w2"), ("rw_a1", "rw_a2"), ("rw_g1", "rw_g2")):
        R = _round_up(p[n1].shape[2], LANES)
        w[n1] = [_pad_cols(x, R).astype(BF16) for x in p[n1]]
        w[n2] = [_pad_rows(x, R).astype(BF16) for x in p[n2]]
    C_KV = p["kv_g_a"].shape[0]
    kv_w_a = p["kv_w_a"]
    w["kv_w_a"] = jnp.concatenate([kv_w_a, _rot_cols(kv_w_a[:, C_KV:])], axis=1).astype(BF16)
    H = p["kv_w_b"].shape[1]
    nope = p["kv_w_b"].shape[2] - LANES
    assert nope == LANES and H % 2 == 0 and kv_w_a.shape[1] == C_KV + ROPE_DIM
    Bn, QL, _ = p["q_w_b"].shape
    qb = p["q_w_b"].reshape(Bn, QL, H, nope + ROPE_DIM)
    qb = jnp.concatenate([qb, _rot_cols(qb[..., nope:])], axis=-1)
    w["q_w_b"] = [qb[l].reshape(QL, H * 256).astype(BF16) for l in range(Bn)]
    wk = p["kv_w_b"][..., :nope]
    wv = p["kv_w_b"][..., nope:]
    w["wk_cols"] = wk.reshape(C_KV, H * LANES).astype(BF16)
    w["wv_cols"] = wv.reshape(C_KV, H * LANES).astype(BF16)
    w["wk_t"] = jnp.transpose(wk, (1, 2, 0)).astype(BF16)
    w["wv_h"] = jnp.transpose(wv, (1, 0, 2)).astype(BF16)
    return w


def _ffn(x, mix, l, p, w, tm):
    x, h = norm_call(x, mix, p["g_mix_post"][l], [p["g_ffn_pre"][l]], [BF16])
    M = x.shape[0]
    F = w["ffn_w_gate"][l].shape[1]
    act = matmul_call(h, [w["ffn_w_gate"][l], w["ffn_w_up"][l]], tm=tm, tn=256, epilogue=_ep_swiglu,
                      name="ffn_gate_up", outs=[((M, F), BF16, (min(tm, M), 256), lambda i, j: (i, j))])[0]
    y = matmul_call(act, [w["ffn_w_down"][l]], tm=min(tm, 512), tn=256, name="ffn_down")[0]
    return x, y


def _rwkv_layer(x, shift_prev, s0, p, w, *, B, T, tm, chunked):
    l = 0
    M, D = x.shape
    (h,) = norm_call(x, pre_gains=[p["g_mix_pre"][l]], pre_dtypes=[F32])
    h3 = h.reshape(B, T, D)
    hp = jnp.concatenate([shift_prev[:, None, :], h3[:, :-1]], axis=1).reshape(M, D)
    shift_out = h3[:, -1]
    m_r, m_w, m_k, m_v, m_a, m_g = mix_call(h, hp, p["rw_mu"][l])
    r = matmul_call(m_r, [w["rw_w_r"][l]], tm=tm, tn=512, name="rwkv_r")[0]
    k = matmul_call(m_k, [w["rw_w_k"][l]], tm=tm, tn=512, name="rwkv_k")[0]
    v = matmul_call(m_v, [w["rw_w_v"][l]], tm=tm, tn=512, name="rwkv_v")[0]
    lw = lora_call(m_w, w["rw_w1"][l], w["rw_w2"][l], p["rw_w0"][l], "decay")
    a = lora_call(m_a, w["rw_a1"][l], w["rw_a2"][l], p["rw_a0"][l], "iclr")
    g = lora_call(m_g, w["rw_g1"][l], w["rw_g2"][l], jnp.zeros((D,), F32), "gate")
    pars = (p["rw_k_k"][l], p["rw_k_a"][l], p["rw_r_k"][l].reshape(D), p["rw_ln_w"][l], p["rw_ln_b"][l])
    if chunked:
        yg, s_new = wkv_chunk_call(r, k, v, lw, a, g, *pars, T=T)
    else:
        tmaj = lambda z: jnp.transpose(z.reshape(B, T, D), (1, 2, 0))
        yg_t, s_t = wkv_step_call(*[tmaj(z) for z in (r, k, v, lw, a, g)], *pars,
                                  jnp.transpose(s0, (1, 2, 3, 0)), T=T)
        yg = jnp.transpose(yg_t, (2, 0, 1)).reshape(M, D)
        s_new = jnp.transpose(s_t, (3, 0, 1, 2))
    mix = matmul_call(yg, [w["rw_w_o"][l]], tm=tm, tn=512, name="rwkv_o")[0]
    return mix, s_new, shift_out


def _mla_common(x_h1, hkv, p, w, j, *, pos0, B, T, tm, scale):
    M = x_h1.shape[0]
    cos, sin = _rope_tables(pos0, T)
    cos = jnp.tile(cos, (B, 1))
    sin = jnp.tile(sin, (B, 1))
    ckv, kpe, ckv_b, kpe_b = kv_call(hkv, w["kv_w_a"], p["kv_g_a"], cos, sin)
    QL = w["q_w_a"][j].shape[1]
    tma = min(tm, 512, M)
    qa = matmul_call(x_h1, [w["q_w_a"][j]], tm=tma, tn=QL, epilogue=_ep_rmsnorm, name="q_down",
                     extras=[(p["q_g_a"][j].reshape(1, QL), (1, QL), lambda i, jj: (0, 0))],
                     outs=[((M, QL), BF16, (tma, QL), lambda i, jj: (i, 0))])[0]
    HN = w["q_w_b"][j].shape[1]
    tmq = min(tm, M)
    q = matmul_call(qa, [w["q_w_b"][j]], tm=tmq, tn=256, epilogue=functools.partial(_ep_rope_q, scale=scale),
                    name="q_up_rope",
                    extras=[(cos, (tmq, LANES), lambda i, jj: (i, 0)), (sin, (tmq, LANES), lambda i, jj: (i, 0))],
                    outs=[((M, HN), BF16, (tmq, 256), lambda i, jj: (i, jj))])[0]
    return ckv, kpe, ckv_b, kpe_b, q


def kernel(x_prompt, x_sample, state_wkv, state_shift, cache_ckv, cache_kpe, page_table, g_mix_pre, g_mix_post,
           g_ffn_pre, g_ffn_post, rw_mu, rw_w_r, rw_w_k, rw_w_v, rw_w_o, rw_w0, rw_w1, rw_w2, rw_a0, rw_a1, rw_a2,
           rw_g1, rw_g2, rw_k_k, rw_k_a, rw_r_k, rw_ln_w, rw_ln_b, kv_g_in, kv_w_a, kv_g_a, kv_w_b, q_w_a, q_g_a,
           q_w_b, mla_w_o, ffn_w_gate, ffn_w_up, ffn_w_down):
    p = dict(g_mix_pre=g_mix_pre, g_mix_post=g_mix_post, g_ffn_pre=g_ffn_pre, g_ffn_post=g_ffn_post,
             rw_mu=rw_mu, rw_w_r=rw_w_r, rw_w_k=rw_w_k, rw_w_v=rw_w_v, rw_w_o=rw_w_o,
             rw_w0=rw_w0, rw_w1=rw_w1, rw_w2=rw_w2, rw_a0=rw_a0, rw_a1=rw_a1, rw_a2=rw_a2,
             rw_g1=rw_g1, rw_g2=rw_g2, rw_k_k=rw_k_k, rw_k_a=rw_k_a, rw_r_k=rw_r_k,
             rw_ln_w=rw_ln_w, rw_ln_b=rw_ln_b, kv_g_in=kv_g_in, kv_w_a=kv_w_a, kv_g_a=kv_g_a, kv_w_b=kv_w_b,
             q_w_a=q_w_a, q_g_a=q_g_a, q_w_b=q_w_b, mla_w_o=mla_w_o,
             ffn_w_gate=ffn_w_gate, ffn_w_up=ffn_w_up, ffn_w_down=ffn_w_down)
    assert g_mix_pre.shape[0] == 2 and rw_mu.shape[0] == 1 and q_w_a.shape[0] == 1, "one RWKV-7 + one MLA layer"
    assert cache_ckv.shape[1] == PAGE_SIZE
    w = _prep_weights(p)
    D = x_prompt.shape[-1]
    H = kv_w_b.shape[1]
    C_KV = kv_g_a.shape[0]
    scale = float((LANES + ROPE_DIM) ** -0.5)
    outs = {}
    for grp in ("prompt", "sample"):
        if grp == "prompt":
            x3 = x_prompt
            B, T, _ = x3.shape
            s0 = None
            shift0 = jnp.zeros((B, D), F32)
            pos0 = 0
        else:
            x3 = x_sample
            B, T, _ = x3.shape
            s0 = state_wkv[0]
            shift0 = state_shift[0]
            pos0 = page_table.shape[1] * PAGE_SIZE
        M = B * T
        tm = _pick(M, (1024, 512, 256, 128, 64, 32, 16, 8))
        x = x3.reshape(M, D)
        mix, s_new, shift_out = _rwkv_layer(x, shift0, s0, p, w, B=B, T=T, tm=tm, chunked=(grp == "prompt"))
        x, y = _ffn(x, mix, 0, p, w, tm)
        x, h1, hkv = norm_call(x, y, g_ffn_post[0], [g_mix_pre[1], kv_g_in], [BF16, BF16])
        ckv, kpe, ckv_b, kpe_b, q = _mla_common(h1, hkv, p, w, 0, pos0=pos0, B=B, T=T, tm=tm, scale=scale)
        if grp == "prompt":
            tme = min(tm, M)
            k_all, v_all = matmul_call(
                ckv_b, [w["wk_cols"], w["wv_cols"]], tm=tme, tn=2 * LANES, epilogue=_ep_expand_kv, name="kv_expand",
                extras=[(kpe_b, (tme, LANES), lambda i, jj: (i, 0))],
                outs=[((M, H * 256), BF16, (tme, 512), lambda i, jj: (i, jj)),
                      ((M, H * LANES), BF16, (tme, 2 * LANES), lambda i, jj: (i, jj))])
            o = attn_prompt_call(q, k_all, v_all, B=B, T=T, H=H)
        else:
            qcat = absorb_q_call(q, w["wk_t"], H=H).reshape(B, T * H, C_KV + LANES)
            pad8 = lambda z: jnp.pad(z.reshape(B, T, z.shape[-1]), ((0, 0), (0, 8 - T), (0, 0)))
            o_lat = paged_attn_call(qcat, pad8(ckv_b), pad8(kpe_b), cache_ckv, jnp.transpose(cache_kpe, (0, 2, 1)),
                                    page_table, H=H)
            o = unabsorb_call(o_lat.reshape(M, H * C_KV), w["wv_h"], H=H)
        mix = matmul_call(o, [w["mla_w_o"][0]], tm=min(tm, 512), tn=512, name="mla_out")[0]
        x, y = _ffn(x, mix, 1, p, w, tm)
        (x,) = norm_call(x, y, g_ffn_post[1])
        outs[grp] = (x.reshape(B, T, D), s_new[None], shift_out[None], ckv.reshape(B, T, C_KV),
                     kpe.reshape(B, T, ROPE_DIM))
    yp, wp, sp, cp, kp = outs["prompt"]
    ys, ws, ss, cs, ks = outs["sample"]
    return (yp, ys, wp, sp, cp, kp, ws, ss, cs, ks)
```

```python
import functools

import jax
import jax.numpy as jnp
from jax import lax
from jax.experimental import pallas as pl
from jax.experimental.pallas import tpu as pltpu

F32 = jnp.float32
BF16 = jnp.bfloat16

V7X_VMEM_BYTES = 64 * 1024 * 1024
LANES = 128
VMEM_LIMIT = V7X_VMEM_BYTES - 8 * 1024 * 1024

HEAD_A = 64
WKV_CHUNK = 64
WKV_GROUP = 256
GN_EPS = 64e-5
RMS_EPS = 1e-6
ROPE_THETA = 10000.0
ROPE_DIM = 64
PAGE_SIZE = 128
NEG_INF = float("-inf")


def _cparams(sem):
    return pltpu.CompilerParams(dimension_semantics=sem, vmem_limit_bytes=VMEM_LIMIT)


def _pick(n, cands):
    for c in cands:
        if n % c == 0:
            return c
    return n


def _rms(x, g):
    return x * lax.rsqrt(jnp.mean(x * x, axis=-1, keepdims=True) + RMS_EPS) * g


_NN = (((1,), (0,)), ((), ()))
_NT = (((1,), (1,)), ((), ()))
_TN = (((0,), (0,)), ((), ()))


def _dg(a, b, dims):
    return lax.dot_general(a, b, dims, preferred_element_type=F32)


def _iota2(shape, dim):
    return lax.broadcasted_iota(jnp.int32, shape, dim)


def _norm_kernel(*refs, has_add, n_pre, pre_dtypes):
    it = iter(refs)
    x_ref = next(it)
    if has_add:
        mix_ref = next(it)
        gpost_ref = next(it)
    gpre_refs = [next(it) for _ in range(n_pre)]
    x = x_ref[...]
    if has_add:
        x = x + _rms(mix_ref[...], gpost_ref[...])
        xo_ref = next(it)
        xo_ref[...] = x
    if n_pre:
        xn = x * lax.rsqrt(jnp.mean(x * x, axis=-1, keepdims=True) + RMS_EPS)
        for g_ref, dt in zip(gpre_refs, pre_dtypes):
            o_ref = next(it)
            o_ref[...] = (xn * g_ref[...]).astype(dt)


def norm_call(x, mix=None, g_post=None, pre_gains=(), pre_dtypes=()):
    M, D = x.shape
    tr = _pick(M, (256, 128, 64, 32, 16, 8))
    row = pl.BlockSpec((tr, D), lambda i: (i, 0))
    vec = pl.BlockSpec((1, D), lambda i: (0, 0))
    has_add = mix is not None
    ins, in_specs = [x], [row]
    if has_add:
        ins += [mix, g_post.reshape(1, D)]
        in_specs += [row, vec]
    for g in pre_gains:
        ins.append(g.reshape(1, D))
        in_specs.append(vec)
    out_shape, out_specs = [], []
    if has_add:
        out_shape.append(jax.ShapeDtypeStruct((M, D), F32))
        out_specs.append(row)
    for dt in pre_dtypes:
        out_shape.append(jax.ShapeDtypeStruct((M, D), dt))
        out_specs.append(row)
    return pl.pallas_call(
        functools.partial(_norm_kernel, has_add=has_add, n_pre=len(pre_gains), pre_dtypes=tuple(pre_dtypes)),
        grid=(M // tr,), in_specs=in_specs, out_specs=out_specs, out_shape=out_shape,
        compiler_params=_cparams(("parallel",)), name="rmsnorm",
    )(*ins)


def _mix_kernel(h_ref, hp_ref, mu_ref, *o_refs):
    h = h_ref[...]
    dx = hp_ref[...] - h
    for j, o_ref in enumerate(o_refs):
        o_ref[...] = (h + dx * mu_ref[j:j + 1, :]).astype(BF16)


def mix_call(h, hp, mu):
    M, D = h.shape
    n = mu.shape[0]
    tr = _pick(M, (256, 128, 64, 32, 16, 8))
    row = pl.BlockSpec((tr, D), lambda i: (i, 0))
    return pl.pallas_call(
        _mix_kernel, grid=(M // tr,),
        in_specs=[row, row, pl.BlockSpec((n, D), lambda i: (0, 0))],
        out_specs=[row] * n, out_shape=[jax.ShapeDtypeStruct((M, D), BF16)] * n,
        compiler_params=_cparams(("parallel",)), name="token_shift_mix",
    )(h, hp, mu)


def _matmul_kernel(a_ref, *refs, nb, ne, epilogue):
    b_refs, e_refs, o_refs = refs[:nb], refs[nb:nb + ne], refs[nb + ne:]
    a = a_ref[...]
    accs = [jnp.dot(a, b[...], preferred_element_type=F32) for b in b_refs]
    epilogue(accs, e_refs, o_refs)


def _ep_store(accs, e_refs, o_refs):
    o_refs[0][...] = accs[0].astype(o_refs[0].dtype)


def _ep_swiglu(accs, e_refs, o_refs):
    g, u = accs
    o_refs[0][...] = (g * jax.nn.sigmoid(g) * u).astype(o_refs[0].dtype)


def matmul_call(a, bs, *, tm, tn, name, epilogue=_ep_store, extras=(), outs=None):
    M, K = a.shape
    bs = [b if isinstance(b, tuple) else (b[None], 0) for b in bs]
    N = bs[0][0].shape[2]
    tm, tn = min(tm, M), min(tn, N)
    assert M % tm == 0 and N % tn == 0, (M, N, tm, tn)
    if outs is None:
        outs = [((M, N), F32, (tm, tn), lambda i, j: (i, j))]
    in_specs = [pl.BlockSpec((tm, K), lambda i, j: (i, 0))]
    in_specs += [pl.BlockSpec((None, K, tn), lambda i, j, l=l: (l, 0, j)) for _, l in bs]
    bs = [b for b, _ in bs]
    in_specs += [pl.BlockSpec(bshape, imap) for _, bshape, imap in extras]
    return pl.pallas_call(
        functools.partial(_matmul_kernel, nb=len(bs), ne=len(extras), epilogue=epilogue),
        grid=(M // tm, N // tn), in_specs=in_specs,
        out_specs=[pl.BlockSpec(bshape, imap) for _, _, bshape, imap in outs],
        out_shape=[jax.ShapeDtypeStruct(shape, dt) for shape, dt, _, _ in outs],
        compiler_params=_cparams(("parallel", "arbitrary")), name=name,
    )(a, *bs, *[e[0] for e in extras])


def _lora_kernel(a_ref, w1_ref, w2_ref, bias_ref, o_ref, *, kind):
    z = jnp.dot(a_ref[...], w1_ref[...], preferred_element_type=F32)
    if kind == "decay":
        z = jnp.tanh(z)
    elif kind == "gate":
        z = jax.nn.sigmoid(z)
    y = jnp.dot(z.astype(BF16), w2_ref[...], preferred_element_type=F32)
    if kind == "decay":
        y = -jnp.exp(-jax.nn.softplus(-(bias_ref[...] + y)) - 0.5)
    elif kind == "iclr":
        y = jax.nn.sigmoid(bias_ref[...] + y)
    o_ref[...] = y


def lora_call(a, w1, w2, bias, kind):
    M, D = a.shape
    R = w1.shape[1]
    tm = _pick(M, (512, 256, 128, 64, 32, 16, 8))
    return pl.pallas_call(
        functools.partial(_lora_kernel, kind=kind), grid=(M // tm,),
        in_specs=[pl.BlockSpec((tm, D), lambda i: (i, 0)), pl.BlockSpec((D, R), lambda i: (0, 0)),
                  pl.BlockSpec((R, D), lambda i: (0, 0)), pl.BlockSpec((1, D), lambda i: (0, 0))],
        out_specs=pl.BlockSpec((tm, D), lambda i: (i, 0)),
        out_shape=jax.ShapeDtypeStruct((M, D), F32),
        compiler_params=_cparams(("parallel",)), name="lora_" + kind,
    )(a, w1, w2, bias.reshape(1, D))


def _split2(x):
    hi = x.astype(BF16)
    lo = (x - hi.astype(F32)).astype(BF16)
    return hi, lo


def _split3(x):
    h1 = x.astype(BF16)
    r1 = x - h1.astype(F32)
    h2 = r1.astype(BF16)
    h3 = (r1 - h2.astype(F32)).astype(BF16)
    return h1, h2, h3


def _mm_exact3(a, b_bf16, dims):
    h1, h2, h3 = _split3(a)
    return _dg(h1, b_bf16, dims) + (_dg(h2, b_bf16, dims) + _dg(h3, b_bf16, dims))


def _segsum(x, ones_bd):
    return _dg(x.astype(BF16), ones_bd, _NN)


def _wkv_groups(r, k, v, lw, a, g, kkp, kap, rkp, lnw, lnb, s_bd, ones_bd, tri):
    C = WKV_CHUNK
    G = WKV_GROUP // HEAD_A

    def each(f, *ls):
        return [f(*xs) for xs in zip(*ls)]

    def bd(x):
        return jnp.concatenate([x.astype(BF16)] * G, axis=0) * ones_bd

    def mm(x, y_bd):
        return _dg(x.astype(BF16), y_bd, _NN)

    def cat(x, y):
        return jnp.concatenate([x, y], axis=0)

    kk = each(lambda k_, p_: k_ * p_, k, kkp)
    ssq = each(lambda x: _segsum(x * x, ones_bd), kk)
    kk = each(lambda x, s: x * lax.rsqrt(jnp.maximum(s, 1e-24)), kk, ssq)
    kmod = each(lambda k_, a_, p_: k_ * (1.0 + (a_ - 1.0) * p_), k, a, kap)
    b = each(lambda x, a_: x * a_, kk, a)

    def cumsum(x):
        hi, lo = _split2(x)
        return _dg(tri, hi, _NN) + _dg(tri, lo, _NN)

    L = each(cumsum, lw)
    Ltot = each(lambda x: x[C - 1:C, :], L)
    enL = each(lambda x: jnp.exp(-x), L)
    eR = each(lambda t_, x: jnp.exp(t_ - x), Ltot, L)
    at = each(lambda x, l_, w_: -x * jnp.exp(l_ - w_), kk, L, lw)
    bt = each(lambda x, e: x * e, b, enL)
    kt = each(lambda x, e: x * e, kmod, enL)
    rt = each(lambda x, l_: x * jnp.exp(l_), r, L)
    bh = each(lambda x, e: x * e, b, eR)
    kh = each(lambda x, e: x * e, kmod, eR)

    t_i = _iota2((C, WKV_GROUP), 0)
    s_i = _iota2((C, WKV_GROUP), 1) % C
    lhs2 = each(lambda x, y: cat(x, y).astype(BF16), at, rt)
    p_b = each(lambda x, y: _dg(x, bd(y), _NT), lhs2, bt)
    p_k = each(lambda x, y: _dg(x, bd(y), _NT), lhs2, kt)
    a_ab = each(lambda x: jnp.where(s_i < t_i, x[:C], 0.0), p_b)
    a_rb = each(lambda x: jnp.where(s_i <= t_i, x[C:], 0.0), p_b)
    a_ak = each(lambda x: jnp.where(s_i < t_i, x[:C], 0.0), p_k)
    a_rk = each(lambda x: jnp.where(s_i <= t_i, x[C:], 0.0), p_k)

    eye = jnp.where(s_i == t_i, 1.0, 0.0)
    tinv = each(lambda x: eye + x, a_ab)
    nk = each(lambda x: mm(x, bd(x)), a_ab)
    for _ in range(4):
        both = each(lambda n_, t_: mm(cat(n_, t_), bd(n_)), nk, tinv)
        tinv = each(lambda t_, x: t_ + x[C:], tinv, both)
        nk = each(lambda x: x[:C], both)
    tinv = each(lambda t_, n_: t_ + mm(t_, bd(n_)), tinv, nk)

    sprod = each(lambda x, s: _dg(x, s.astype(BF16), _NT), lhs2, s_bd)
    akrk = each(lambda x, y, v_: mm(cat(x, y), bd(v_)), a_ak, a_rk, v)
    u = each(lambda t_, s, x: mm(t_, bd(s[:C] + x[:C])), tinv, sprod, akrk)
    o = each(lambda s, x, m_, u_: s[C:] + x[C:] + mm(m_, bd(u_)), sprod, akrk, a_rb, u)

    upd = each(lambda u_, v_, b_, k_: _dg(cat(u_, v_).astype(BF16), cat(b_, k_).astype(BF16), _TN),
               u, v, bh, kh)
    mask_f = ones_bd.astype(F32)
    s_new = each(lambda s, t_, x: s * jnp.exp(t_) + x * mask_f, s_bd, Ltot, upd)

    inv_n = 1.0 / HEAD_A
    sums = each(lambda o_, r_, k_, p_: _segsum(cat(o_, r_ * k_ * p_), ones_bd), o, r, kmod, rkp)
    d = each(lambda o_, s: o_ - s[:C] * inv_n, o, sums)
    var = each(lambda x: _segsum(x * x, ones_bd) * inv_n, d)
    y = each(lambda d_, var_, w_, b_, s, v_, g_: (d_ * lax.rsqrt(var_ + GN_EPS) * w_ + b_ + s[C:] * v_) * g_,
             d, var, lnw, lnb, sums, v, g)
    return y, s_new


def _wkv_chunk_kernel(r_ref, k_ref, v_ref, lw_ref, a_ref, g_ref, kk_ref, ka_ref, rk_ref, lnw_ref, lnb_ref,
                      y_ref, st_ref, sbd_ref):
    C = WKV_CHUNK
    ng = r_ref.shape[1] // WKV_GROUP
    c_idx = pl.program_id(2)
    W = WKV_GROUP
    same_head = (_iota2((W, W), 0) // HEAD_A) == (_iota2((W, W), 1) // HEAD_A)
    ones_bd = jnp.where(same_head, 1.0, 0.0).astype(BF16)
    tri = jnp.where(_iota2((C, C), 1) <= _iota2((C, C), 0), 1.0, 0.0).astype(BF16)

    @pl.when(c_idx == 0)
    def _():
        sbd_ref[...] = jnp.zeros(sbd_ref.shape, F32)

    sls = [slice(W * q, W * (q + 1)) for q in range(ng)]
    split = lambda ref: [ref[:, sl] for sl in sls]
    ys, s_news = _wkv_groups(*[split(ref) for ref in (r_ref, k_ref, v_ref, lw_ref, a_ref, g_ref, kk_ref, ka_ref,
                                                      rk_ref, lnw_ref, lnb_ref)],
                             [sbd_ref[q] for q in range(ng)], ones_bd, tri)
    for q in range(ng):
        y_ref[:, sls[q]] = ys[q].astype(y_ref.dtype)
        sbd_ref[q] = s_news[q]

    @pl.when(c_idx == pl.num_programs(2) - 1)
    def _():
        fold = jnp.where((_iota2((HEAD_A, W), 1) % HEAD_A) == _iota2((HEAD_A, W), 0), 1.0, 0.0).astype(BF16)
        for q in range(ng):
            rows = _mm_exact3(sbd_ref[q], fold, _NT)
            st_ref[0, 4 * q:4 * q + 4] = rows.reshape(4, HEAD_A, HEAD_A)


def wkv_chunk_call(r, k, v, lw, a, g, k_k, k_a, r_k, ln_w, ln_b, *, T):
    M, D = r.shape
    B = M // T
    H = D // HEAD_A
    C = WKV_CHUNK
    assert C == HEAD_A and T % C == 0
    Lw = _pick(D, (4 * WKV_GROUP, 2 * WKV_GROUP, WKV_GROUP))
    assert D % Lw == 0 and Lw % WKV_GROUP == 0
    hb = Lw // HEAD_A
    nc = T // C
    row = pl.BlockSpec((C, Lw), lambda bi, hg, c: (bi * nc + c, hg))
    vec = pl.BlockSpec((1, Lw), lambda bi, hg, c: (0, hg))
    st = pl.BlockSpec((1, hb, HEAD_A, HEAD_A), lambda bi, hg, c: (bi, hg, 0, 0))
    vecs = [x.reshape(1, D) for x in (k_k, k_a, r_k, ln_w, ln_b)]
    return pl.pallas_call(
        _wkv_chunk_kernel,
        grid=(B, D // Lw, nc),
        in_specs=[row] * 6 + [vec] * 5,
        out_specs=[row, st],
        out_shape=[jax.ShapeDtypeStruct((M, D), BF16), jax.ShapeDtypeStruct((B, H, HEAD_A, HEAD_A), F32)],
        scratch_shapes=[pltpu.VMEM((Lw // WKV_GROUP, WKV_GROUP, WKV_GROUP), F32)],
        compiler_params=_cparams(("parallel", "parallel", "arbitrary")), name="wkv_chunk",
    )(r, k, v, lw, a, g, *vecs)


def _wkv_step_kernel(r_ref, k_ref, v_ref, lw_ref, a_ref, g_ref, kk_ref, ka_ref, rk_ref, lnw_ref, lnb_ref,
                     s_ref, y_ref, so_ref, op_ref, sa_ref, o_ref, *, T, hb):
    N = HEAD_A
    for h in range(hb):
        sl = slice(h * N, (h + 1) * N)
        for t in range(T):
            k = k_ref[t, sl, :]
            a = a_ref[t, sl, :]
            kk = k * kk_ref[sl, :]
            kk = kk * lax.rsqrt(jnp.maximum(jnp.sum(kk * kk, axis=0, keepdims=True), 1e-24))
            op_ref[t, 0] = jnp.exp(lw_ref[t, sl, :])
            op_ref[t, 1] = kk * a
            op_ref[t, 2] = k * (1.0 + (a - 1.0) * ka_ref[sl, :])
            op_ref[t, 3] = r_ref[t, sl, :]
            op_ref[t, 4] = -kk

        def first(vi, carry, h=h):
            sa_ref[0, pl.ds(vi, 1), :] = jnp.sum(s_ref[h, vi] * op_ref[0, 4], axis=0, keepdims=True)
            return carry

        lax.fori_loop(0, N, first, 0, unroll=8)
        for t in range(T):
            src = s_ref if t == 0 else so_ref
            cur, nxt = t % 2, (t + 1) % 2

            def body(vi, carry, h=h, t=t, src=src, cur=cur, nxt=nxt):
                s_row = src[h, vi]
                sa_row = sa_ref[cur, pl.ds(vi, 1), :]
                v_row = v_ref[t, pl.ds(h * N + vi, 1), :]
                s_new = s_row * op_ref[t, 0] + sa_row * op_ref[t, 1] + v_row * op_ref[t, 2]
                so_ref[h, vi] = s_new
                o_ref[pl.ds(vi, 1), :] = jnp.sum(s_new * op_ref[t, 3], axis=0, keepdims=True)
                if t + 1 < T:
                    sa_ref[nxt, pl.ds(vi, 1), :] = jnp.sum(s_new * op_ref[t + 1, 4], axis=0, keepdims=True)
                return carry

            lax.fori_loop(0, N, body, 0, unroll=2)
            o = o_ref[...]
            d = o - jnp.mean(o, axis=0, keepdims=True)
            var = jnp.mean(d * d, axis=0, keepdims=True)
            o_n = d * lax.rsqrt(var + GN_EPS) * lnw_ref[sl, :] + lnb_ref[sl, :]
            bonus = jnp.sum(op_ref[t, 3] * op_ref[t, 2] * rk_ref[sl, :], axis=0, keepdims=True) * v_ref[t, sl, :]
            y_ref[t, sl, :] = ((o_n + bonus) * g_ref[t, sl, :]).astype(y_ref.dtype)


def wkv_step_call(r, k, v, lw, a, g, k_k, k_a, r_k, ln_w, ln_b, s0, *, T):
    _, D, Bd = r.shape
    H = D // HEAD_A
    hb = 2 if H % 2 == 0 else 1
    seq = pl.BlockSpec((T, hb * HEAD_A, Bd), lambda i: (0, i, 0))
    par = pl.BlockSpec((hb * HEAD_A, Bd), lambda i: (i, 0))
    st = pl.BlockSpec((hb, HEAD_A, HEAD_A, Bd), lambda i: (i, 0, 0, 0))
    pars = [jnp.broadcast_to(x.reshape(D, 1), (D, Bd)) for x in (k_k, k_a, r_k, ln_w, ln_b)]
    return pl.pallas_call(
        functools.partial(_wkv_step_kernel, T=T, hb=hb), grid=(H // hb,),
        in_specs=[seq] * 6 + [par] * 5 + [st],
        out_specs=[seq, st],
        out_shape=[jax.ShapeDtypeStruct((T, D, Bd), BF16), jax.ShapeDtypeStruct((H, HEAD_A, HEAD_A, Bd), F32)],
        scratch_shapes=[pltpu.VMEM((T, 5, HEAD_A, Bd), F32), pltpu.VMEM((2, HEAD_A, Bd), F32),
                        pltpu.VMEM((HEAD_A, Bd), F32)],
        compiler_params=_cparams(("parallel",)), name="wkv_step",
    )(r, k, v, lw, a, g, *pars, s0)


def _rope_tables(pos0, T):
    half = ROPE_DIM // 2
    inv = ROPE_THETA ** (-jnp.arange(half, dtype=F32) / half)
    ang = (pos0 + jnp.arange(T)).astype(F32)[:, None] * inv[None, :]
    cos, sin = jnp.cos(ang), jnp.sin(ang)
    return jnp.tile(cos, (1, LANES // half)), jnp.tile(sin, (1, LANES // half))


def _rot_cols(w):
    half = ROPE_DIM // 2
    return jnp.concatenate([-w[..., half:], w[..., :half]], axis=-1)


def _rope_tail(tail, cos, sin):
    rot = pltpu.roll(tail, ROPE_DIM, 1)
    return jnp.where(_iota2(tail.shape, 1) < ROPE_DIM, tail * cos + rot * sin, 0.0)


def _kv_kernel(a_ref, w_ref, g_ref, cos_ref, sin_ref, ckv_ref, kpe_ref, ckvb_ref, kpeb_ref, *, C_KV):
    kv = jnp.dot(a_ref[...], w_ref[...], preferred_element_type=F32)
    ckv = _rms(kv[:, :C_KV], g_ref[...])
    kpe = _rope_tail(kv[:, C_KV:C_KV + LANES], cos_ref[...], sin_ref[...])
    ckv_ref[...] = ckv
    kpe_ref[...] = kpe[:, :ROPE_DIM]
    ckvb_ref[...] = ckv.astype(BF16)
    kpeb_ref[...] = kpe.astype(BF16)


def kv_call(hkv, w_ext, g_a, cos, sin):
    M, D = hkv.shape
    C_KV = g_a.shape[0]
    NW = w_ext.shape[1]
    tm = _pick(M, (512, 256, 128, 64, 32, 16, 8))
    rowb = lambda n: pl.BlockSpec((tm, n), lambda i: (i, 0))
    return pl.pallas_call(
        functools.partial(_kv_kernel, C_KV=C_KV), grid=(M // tm,),
        in_specs=[rowb(D), pl.BlockSpec((D, NW), lambda i: (0, 0)), pl.BlockSpec((1, C_KV), lambda i: (0, 0)),
                  rowb(LANES), rowb(LANES)],
        out_specs=[rowb(C_KV), rowb(ROPE_DIM), rowb(C_KV), rowb(LANES)],
        out_shape=[jax.ShapeDtypeStruct((M, C_KV), F32), jax.ShapeDtypeStruct((M, ROPE_DIM), F32),
                   jax.ShapeDtypeStruct((M, C_KV), BF16), jax.ShapeDtypeStruct((M, LANES), BF16)],
        compiler_params=_cparams(("parallel",)), name="latent_kv",
    )(hkv, w_ext, g_a.reshape(1, C_KV), cos, sin)


def _ep_rmsnorm(accs, e_refs, o_refs):
    o_refs[0][...] = _rms(accs[0], e_refs[0][...]).astype(o_refs[0].dtype)


def _ep_rope_q(accs, e_refs, o_refs, *, scale):
    acc = accs[0]
    cos, sin = e_refs[0][...], e_refs[1][...]
    pieces = []
    for h in range(acc.shape[1] // 256):
        pieces += [acc[:, 256 * h:256 * h + LANES], _rope_tail(acc[:, 256 * h + LANES:256 * (h + 1)], cos, sin)]
    o_refs[0][...] = (jnp.concatenate(pieces, axis=1) * scale).astype(o_refs[0].dtype)


def _ep_expand_kv(accs, e_refs, o_refs):
    k_nope = accs[0].astype(BF16)
    pe = e_refs[0][...]
    pieces = []
    for h in range(k_nope.shape[1] // LANES):
        pieces += [k_nope[:, LANES * h:LANES * (h + 1)], pe]
    o_refs[0][...] = jnp.concatenate(pieces, axis=1)
    o_refs[1][...] = accs[1].astype(BF16)


def _attn_prompt_kernel(q_ref, k_ref, v_ref, o_ref, *, tq):
    T = q_ref.shape[0]
    causal = _iota2((tq, tq), 1) <= _iota2((tq, tq), 0)
    for qi in range(T // tq):
        n = (qi + 1) * tq
        q = q_ref[qi * tq:(qi + 1) * tq, :]
        s = _dg(q, k_ref[:n, :], _NT)
        diag = jnp.where(causal, s[:, n - tq:], NEG_INF)
        s = diag if qi == 0 else jnp.concatenate([s[:, :n - tq], diag], axis=1)
        m = jnp.max(s, axis=-1, keepdims=True)
        p = jnp.exp(s - m)
        l = jnp.sum(p, axis=-1, keepdims=True)
        o = jnp.dot(p.astype(BF16), v_ref[:n, :], preferred_element_type=F32)
        o_ref[qi * tq:(qi + 1) * tq, :] = (o / l).astype(o_ref.dtype)


def attn_prompt_call(q, k, v, *, B, T, H):
    tq = _pick(T, (512, 256, 128))
    assert T % tq == 0 and tq % LANES == 0
    return pl.pallas_call(
        functools.partial(_attn_prompt_kernel, tq=tq),
        grid=(B, H),
        in_specs=[pl.BlockSpec((T, 256), lambda b, h: (b, h)), pl.BlockSpec((T, 256), lambda b, h: (b, h)),
                  pl.BlockSpec((T, LANES), lambda b, h: (b, h))],
        out_specs=pl.BlockSpec((T, LANES), lambda b, h: (b, h)),
        out_shape=jax.ShapeDtypeStruct((B * T, H * LANES), BF16),
        compiler_params=_cparams(("parallel", "parallel")), name="attn_prompt",
    )(q, k, v)


def _absorb_q_kernel(q_ref, wk_ref, o_ref):
    q = q_ref[...]
    lat = jnp.dot(q[:, :LANES], wk_ref[0], preferred_element_type=F32)
    o_ref[...] = jnp.concatenate([lat.astype(BF16), q[:, LANES:]], axis=1)


def absorb_q_call(q, wk_t, *, H):
    M = q.shape[0]
    C_KV = wk_t.shape[2]
    return pl.pallas_call(
        _absorb_q_kernel, grid=(H,),
        in_specs=[pl.BlockSpec((M, 256), lambda h: (0, h)), pl.BlockSpec((1, LANES, C_KV), lambda h: (h, 0, 0))],
        out_specs=pl.BlockSpec((M, C_KV + LANES), lambda h: (0, h)),
        out_shape=jax.ShapeDtypeStruct((M, H * (C_KV + LANES)), BF16),
        compiler_params=_cparams(("parallel",)), name="absorb_q",
    )(q, wk_t)


def _paged_attn_kernel(pt_ref, q_ref, cn_ref, kn_ref, ckv_hbm, kpe_hbm, o_ref, ckv_buf, kpe_buf, sem,
                       m_ref, l_ref, acc_ref, *, P, C_KV, H, n_chain):
    b, j = pl.program_id(0), pl.program_id(1)
    nb, nj = pl.num_programs(0), pl.num_programs(1)
    step = b * nj + j
    slot = lax.rem(step, 2)
    is_last = step == nb * nj - 1
    wrap = j + 1 == nj
    b_nxt = jnp.where(is_last, b, jnp.where(wrap, b + 1, b))
    j_nxt = jnp.where(is_last, j, jnp.where(wrap, 0, j + 1))
    per = P // n_chain

    def page_copies(bb, jj, sl, i):
        page = pt_ref[bb, jj * P + i]
        return (pltpu.make_async_copy(ckv_hbm.at[page], ckv_buf.at[sl, i], sem.at[sl, 0]),
                pltpu.make_async_copy(kpe_hbm.at[page], kpe_buf.at[sl, i], sem.at[sl, 1]))

    def start_pages(bb, jj, sl, lo, hi):
        for i in range(lo, hi):
            for cp in page_copies(bb, jj, sl, i):
                cp.start()

    def wait_pages(bb, jj, sl):
        for i in range(P):
            for cp in page_copies(bb, jj, sl, i):
                cp.wait()

    @pl.when(step == 0)
    def _():
        start_pages(b, j, slot, 0, P)

    @pl.when(j == 0)
    def _():
        m_ref[...] = jnp.full(m_ref.shape, NEG_INF, F32)
        l_ref[...] = jnp.zeros(l_ref.shape, F32)
        acc_ref[...] = jnp.zeros(acc_ref.shape, F32)

    q = q_ref[0]
    q_lat, q_pe = q[:, :C_KV], q[:, C_KV:C_KV + ROPE_DIM]

    def update(c, s, vals):
        m_old = m_ref[c]
        m_new = jnp.maximum(m_old, jnp.max(s, axis=-1, keepdims=True))
        corr = jnp.exp(m_old - m_new)
        p = jnp.exp(s - m_new)
        l_ref[c] = l_ref[c] * corr + jnp.sum(p, axis=-1, keepdims=True)
        acc_ref[c] = acc_ref[c] * corr + jnp.dot(p.astype(BF16), vals, preferred_element_type=F32)
        m_ref[c] = m_new

    wait_pages(b, j, slot)
    kcs, scores = [], []
    for c in range(n_chain):
        start_pages(b_nxt, j_nxt, 1 - slot, c * per, (c + 1) * per)
        pages = range(c * per, (c + 1) * per)
        kc = jnp.concatenate([ckv_buf[slot, i].astype(BF16) for i in pages], axis=0)
        kr = jnp.concatenate([kpe_buf[slot, i].astype(BF16) for i in pages], axis=1)
        kcs.append(kc)
        scores.append(_dg(q_lat, kc, _NT) + _dg(q_pe, kr, _NN))
    for c in range(n_chain):
        update(c, scores[c], kcs[c])

    @pl.when(is_last)
    def _():
        wait_pages(b_nxt, j_nxt, 1 - slot)

    @pl.when(j == nj - 1)
    def _():
        cn = cn_ref[0]
        kn = kn_ref[0][:, :ROPE_DIM]
        sn = _dg(q_lat, cn, _NT) + _dg(q_pe, kn, _NT)
        sn = jnp.where(_iota2(sn.shape, 1) <= _iota2(sn.shape, 0) // H, sn, NEG_INF)
        update(0, sn, cn)
        m = m_ref[0]
        for c in range(1, n_chain):
            m = jnp.maximum(m, m_ref[c])
        acc = jnp.zeros(acc_ref.shape[1:], F32)
        l = jnp.zeros(l_ref.shape[1:], F32)
        for c in range(n_chain):
            wgt = jnp.exp(m_ref[c] - m)
            acc = acc + acc_ref[c] * wgt
            l = l + l_ref[c] * wgt
        o_ref[0] = (acc / l).astype(o_ref.dtype)


def paged_attn_call(qcat, ckv_new, kpe_new, cache_ckv, cache_kpe_t, page_table, *, H):
    Bd, R, QW = qcat.shape
    C_KV = QW - LANES
    n_pages = page_table.shape[1]
    P = _pick(n_pages, (32, 16, 8, 4, 2))
    n_chain = _pick(P, (4, 2, 1))
    page = cache_ckv.shape[1]

    in_specs = [pl.BlockSpec((1, R, QW), lambda b, j, pt: (b, 0, 0)),
                pl.BlockSpec((1, 8, C_KV), lambda b, j, pt: (b, 0, 0)),
                pl.BlockSpec((1, 8, LANES), lambda b, j, pt: (b, 0, 0)),
                pl.BlockSpec(memory_space=pl.ANY), pl.BlockSpec(memory_space=pl.ANY)]
    grid_spec = pltpu.PrefetchScalarGridSpec(
        num_scalar_prefetch=1, grid=(Bd, n_pages // P), in_specs=in_specs,
        out_specs=pl.BlockSpec((1, R, C_KV), lambda b, j, pt: (b, 0, 0)),
        scratch_shapes=[pltpu.VMEM((2, P, page, C_KV), F32), pltpu.VMEM((2, P, ROPE_DIM, page), F32),
                        pltpu.SemaphoreType.DMA((2, 2)),
                        pltpu.VMEM((n_chain, R, 1), F32), pltpu.VMEM((n_chain, R, 1), F32),
                        pltpu.VMEM((n_chain, R, C_KV), F32)])
    return pl.pallas_call(
        functools.partial(_paged_attn_kernel, P=P, C_KV=C_KV, H=H, n_chain=n_chain),
        grid_spec=grid_spec, out_shape=jax.ShapeDtypeStruct((Bd, R, C_KV), BF16),
        compiler_params=_cparams(("arbitrary", "arbitrary")), name="paged_attn",
    )(page_table, qcat, ckv_new, kpe_new, cache_ckv, cache_kpe_t)


def _unabsorb_kernel(o_ref, wv_ref, out_ref):
    out_ref[...] = jnp.dot(o_ref[...], wv_ref[0], preferred_element_type=F32).astype(out_ref.dtype)


def unabsorb_call(o_lat, wv, *, H):
    M = o_lat.shape[0]
    C_KV = wv.shape[1]
    return pl.pallas_call(
        _unabsorb_kernel, grid=(H,),
        in_specs=[pl.BlockSpec((M, C_KV), lambda h: (0, h)), pl.BlockSpec((1, C_KV, LANES), lambda h: (h, 0, 0))],
        out_specs=pl.BlockSpec((M, LANES), lambda h: (0, h)),
        out_shape=jax.ShapeDtypeStruct((M, H * LANES), BF16),
        compiler_params=_cparams(("parallel",)), name="unabsorb_o",
    )(o_lat, wv)


def _pad_cols(w, n):
    return w if w.shape[1] == n else jnp.pad(w, ((0, 0), (0, n - w.shape[1])))


def _pad_rows(w, n):
    return w if w.shape[0] == n else jnp.pad(w, ((0, n - w.shape[0]), (0, 0)))


def _round_up(n, m):
    return -(-n // m) * m


def _prep_weights(p):
    w = {}
    for name in ("rw_w_r", "rw_w_k", "rw_w_v", "rw_w_o", "ffn_w_gate", "ffn_w_up", "ffn_w_down", "q_w_a", "mla_w_o"):
        w[name] = p[name].astype(BF16)
    for n1, n2 in (("rw_w1", "rw_w2"), ("rw_a1", "rw_a2"), ("rw_g1", "rw_g2")):
        R = _round_up(p[n1].shape[2], LANES)
        w[n1] = [_pad_cols(x, R).astype(BF16) for x in p[n1]]
        w[n2] = [_pad_rows(x, R).astype(BF16) for x in p[n2]]
    C_KV = p["kv_g_a"].shape[0]
    kv_w_a = p["kv_w_a"]
    w["kv_w_a"] = jnp.concatenate([kv_w_a, _rot_cols(kv_w_a[:, C_KV:])], axis=1).astype(BF16)
    H = p["kv_w_b"].shape[1]
    nope = p["kv_w_b"].shape[2] - LANES
    assert nope == LANES and H % 2 == 0 and kv_w_a.shape[1] == C_KV + ROPE_DIM
    Bn, QL, _ = p["q_w_b"].shape
    qb = p["q_w_b"].reshape(Bn, QL, H, nope + ROPE_DIM)
    qb = jnp.concatenate([qb, _rot_cols(qb[..., nope:])], axis=-1)
    w["q_w_b"] = [qb[l].reshape(QL, H * 256).astype(BF16) for l in range(Bn)]
    wk = p["kv_w_b"][..., :nope]
    wv = p["kv_w_b"][..., nope:]
    w["wk_cols"] = wk.reshape(C_KV, H * LANES).astype(BF16)
    w["wv_cols"] = wv.reshape(C_KV, H * LANES).astype(BF16)
    w["wk_t"] = jnp.transpose(wk, (1, 2, 0)).astype(BF16)
    w["wv_h"] = jnp.transpose(wv, (1, 0, 2)).astype(BF16)
    return w


def _ffn(x, mix, l, p, w, tm):
    x, h = norm_call(x, mix, p["g_mix_post"][l], [p["g_ffn_pre"][l]], [BF16])
    M = x.shape[0]
    F = w["ffn_w_gate"].shape[2]
    act = matmul_call(h, [(w["ffn_w_gate"], l), (w["ffn_w_up"], l)], tm=tm, tn=256, epilogue=_ep_swiglu,
                      name="ffn_gate_up", outs=[((M, F), BF16, (min(tm, M), 256), lambda i, j: (i, j))])[0]
    y = matmul_call(act, [(w["ffn_w_down"], l)], tm=min(tm, 512), tn=512, name="ffn_down")[0]
    return x, y


def _rwkv_layer(x, shift_prev, s0, p, w, *, B, T, tm, chunked):
    l = 0
    M, D = x.shape
    (h,) = norm_call(x, pre_gains=[p["g_mix_pre"][l]], pre_dtypes=[F32])
    h3 = h.reshape(B, T, D)
    hp = jnp.concatenate([shift_prev[:, None, :], h3[:, :-1]], axis=1).reshape(M, D)
    shift_out = h3[:, -1]
    m_r, m_w, m_k, m_v, m_a, m_g = mix_call(h, hp, p["rw_mu"][l])
    r = matmul_call(m_r, [(w["rw_w_r"], l)], tm=tm, tn=512, name="rwkv_r")[0]
    k = matmul_call(m_k, [(w["rw_w_k"], l)], tm=tm, tn=512, name="rwkv_k")[0]
    v = matmul_call(m_v, [(w["rw_w_v"], l)], tm=tm, tn=512, name="rwkv_v")[0]
    lw = lora_call(m_w, w["rw_w1"][l], w["rw_w2"][l], p["rw_w0"][l], "decay")
    a = lora_call(m_a, w["rw_a1"][l], w["rw_a2"][l], p["rw_a0"][l], "iclr")
    g = lora_call(m_g, w["rw_g1"][l], w["rw_g2"][l], jnp.zeros((D,), F32), "gate")
    pars = (p["rw_k_k"][l], p["rw_k_a"][l], p["rw_r_k"][l].reshape(D), p["rw_ln_w"][l], p["rw_ln_b"][l])
    if chunked:
        yg, s_new = wkv_chunk_call(r, k, v, lw, a, g, *pars, T=T)
    else:
        tmaj = lambda z: jnp.transpose(z.reshape(B, T, D), (1, 2, 0))
        yg_t, s_t = wkv_step_call(*[tmaj(z) for z in (r, k, v, lw, a, g)], *pars,
                                  jnp.transpose(s0, (1, 2, 3, 0)), T=T)
        yg = jnp.transpose(yg_t, (2, 0, 1)).reshape(M, D)
        s_new = jnp.transpose(s_t, (3, 0, 1, 2))
    mix = matmul_call(yg, [(w["rw_w_o"], l)], tm=tm, tn=512, name="rwkv_o")[0]
    return mix, s_new, shift_out


def _mla_common(x_h1, hkv, p, w, j, *, pos0, B, T, tm, scale):
    M = x_h1.shape[0]
    cos, sin = _rope_tables(pos0, T)
    cos = jnp.tile(cos, (B, 1))
    sin = jnp.tile(sin, (B, 1))
    ckv, kpe, ckv_b, kpe_b = kv_call(hkv, w["kv_w_a"], p["kv_g_a"], cos, sin)
    QL = w["q_w_a"].shape[2]
    tma = min(tm, 512, M)
    qa = matmul_call(x_h1, [(w["q_w_a"], j)], tm=tma, tn=QL, epilogue=_ep_rmsnorm, name="q_down",
                     extras=[(p["q_g_a"][j].reshape(1, QL), (1, QL), lambda i, jj: (0, 0))],
                     outs=[((M, QL), BF16, (tma, QL), lambda i, jj: (i, 0))])[0]
    HN = w["q_w_b"][j].shape[1]
    tmq = min(tm, M)
    tnq = _pick(HN, (1024, 512, 256))
    q = matmul_call(qa, [w["q_w_b"][j]], tm=tmq, tn=tnq, epilogue=functools.partial(_ep_rope_q, scale=scale),
                    name="q_up_rope",
                    extras=[(cos, (tmq, LANES), lambda i, jj: (i, 0)), (sin, (tmq, LANES), lambda i, jj: (i, 0))],
                    outs=[((M, HN), BF16, (tmq, tnq), lambda i, jj: (i, jj))])[0]
    return ckv, kpe, ckv_b, kpe_b, q


def kernel(x_prompt, x_sample, state_wkv, state_shift, cache_ckv, cache_kpe, page_table, g_mix_pre, g_mix_post,
           g_ffn_pre, g_ffn_post, rw_mu, rw_w_r, rw_w_k, rw_w_v, rw_w_o, rw_w0, rw_w1, rw_w2, rw_a0, rw_a1, rw_a2,
           rw_g1, rw_g2, rw_k_k, rw_k_a, rw_r_k, rw_ln_w, rw_ln_b, kv_g_in, kv_w_a, kv_g_a, kv_w_b, q_w_a, q_g_a,
           q_w_b, mla_w_o, ffn_w_gate, ffn_w_up, ffn_w_down):
    p = dict(g_mix_pre=g_mix_pre, g_mix_post=g_mix_post, g_ffn_pre=g_ffn_pre, g_ffn_post=g_ffn_post,
             rw_mu=rw_mu, rw_w_r=rw_w_r, rw_w_k=rw_w_k, rw_w_v=rw_w_v, rw_w_o=rw_w_o,
             rw_w0=rw_w0, rw_w1=rw_w1, rw_w2=rw_w2, rw_a0=rw_a0, rw_a1=rw_a1, rw_a2=rw_a2,
             rw_g1=rw_g1, rw_g2=rw_g2, rw_k_k=rw_k_k, rw_k_a=rw_k_a, rw_r_k=rw_r_k,
             rw_ln_w=rw_ln_w, rw_ln_b=rw_ln_b, kv_g_in=kv_g_in, kv_w_a=kv_w_a, kv_g_a=kv_g_a, kv_w_b=kv_w_b,
             q_w_a=q_w_a, q_g_a=q_g_a, q_w_b=q_w_b, mla_w_o=mla_w_o,
             ffn_w_gate=ffn_w_gate, ffn_w_up=ffn_w_up, ffn_w_down=ffn_w_down)
    assert g_mix_pre.shape[0] == 2 and rw_mu.shape[0] == 1 and q_w_a.shape[0] == 1, "one RWKV-7 + one MLA layer"
    assert cache_ckv.shape[1] == PAGE_SIZE
    w = _prep_weights(p)
    D = x_prompt.shape[-1]
    H = kv_w_b.shape[1]
    C_KV = kv_g_a.shape[0]
    scale = float((LANES + ROPE_DIM) ** -0.5)
    outs = {}
    for grp in ("prompt", "sample"):
        if grp == "prompt":
            x3 = x_prompt
            B, T, _ = x3.shape
            s0 = None
            shift0 = jnp.zeros((B, D), F32)
            pos0 = 0
        else:
            x3 = x_sample
            B, T, _ = x3.shape
            s0 = state_wkv[0]
            shift0 = state_shift[0]
            pos0 = page_table.shape[1] * PAGE_SIZE
        M = B * T
        tm = _pick(M, (1024, 512, 256, 128, 64, 32, 16, 8))
        x = x3.reshape(M, D)
        mix, s_new, shift_out = _rwkv_layer(x, shift0, s0, p, w, B=B, T=T, tm=tm, chunked=(grp == "prompt"))
        x, y = _ffn(x, mix, 0, p, w, tm)
        x, h1, hkv = norm_call(x, y, g_ffn_post[0], [g_mix_pre[1], kv_g_in], [BF16, BF16])
        ckv, kpe, ckv_b, kpe_b, q = _mla_common(h1, hkv, p, w, 0, pos0=pos0, B=B, T=T, tm=tm, scale=scale)
        if grp == "prompt":
            tme = min(tm, M)
            tne = _pick(H * LANES, (1024, 512, 256))
            k_all, v_all = matmul_call(
                ckv_b, [w["wk_cols"], w["wv_cols"]], tm=tme, tn=tne, epilogue=_ep_expand_kv, name="kv_expand",
                extras=[(kpe_b, (tme, LANES), lambda i, jj: (i, 0))],
                outs=[((M, H * 256), BF16, (tme, 2 * tne), lambda i, jj: (i, jj)),
                      ((M, H * LANES), BF16, (tme, tne), lambda i, jj: (i, jj))])
            o = attn_prompt_call(q, k_all, v_all, B=B, T=T, H=H)
        else:
            qcat = absorb_q_call(q, w["wk_t"], H=H).reshape(B, T * H, C_KV + LANES)
            pad8 = lambda z: jnp.pad(z.reshape(B, T, z.shape[-1]), ((0, 0), (0, 8 - T), (0, 0)))
            o_lat = paged_attn_call(qcat, pad8(ckv_b), pad8(kpe_b), cache_ckv, jnp.transpose(cache_kpe, (0, 2, 1)),
                                    page_table, H=H)
            o = unabsorb_call(o_lat.reshape(M, H * C_KV), w["wv_h"], H=H)
        mix = matmul_call(o, [(w["mla_w_o"], 0)], tm=min(tm, 512), tn=512, name="mla_out")[0]
        x, y = _ffn(x, mix, 1, p, w, tm)
        (x,) = norm_call(x, y, g_ffn_post[1])
        outs[grp] = (x.reshape(B, T, D), s_new[None], shift_out[None], ckv.reshape(B, T, C_KV),
                     kpe.reshape(B, T, ROPE_DIM))
    yp, wp, sp, cp, kp = outs["prompt"]
    ys, ws, ss, cs, ks = outs["sample"]
    return (yp, ys, wp, sp, cp, kp, ws, ss, cs, ks)
```

```python
import functools

import jax
import jax.numpy as jnp
from jax import lax
from jax.experimental import pallas as pl
from jax.experimental.pallas import tpu as pltpu

F32 = jnp.float32
BF16 = jnp.bfloat16

V7X_VMEM_BYTES = 64 * 1024 * 1024
LANES = 128
VMEM_LIMIT = V7X_VMEM_BYTES - 8 * 1024 * 1024

HEAD_A = 64
WKV_CHUNK = 64
WKV_GROUP = 256
GN_EPS = 64e-5
RMS_EPS = 1e-6
ROPE_THETA = 10000.0
ROPE_DIM = 64
PAGE_SIZE = 128
NEG_INF = float("-inf")


def _cparams(sem):
    return pltpu.CompilerParams(dimension_semantics=sem, vmem_limit_bytes=VMEM_LIMIT)


def _pick(n, cands):
    for c in cands:
        if n % c == 0:
            return c
    return n


def _rms(x, g):
    return x * lax.rsqrt(jnp.mean(x * x, axis=-1, keepdims=True) + RMS_EPS) * g


_NN = (((1,), (0,)), ((), ()))
_NT = (((1,), (1,)), ((), ()))
_TN = (((0,), (0,)), ((), ()))


def _dg(a, b, dims):
    return lax.dot_general(a, b, dims, preferred_element_type=F32)


def _iota2(shape, dim):
    return lax.broadcasted_iota(jnp.int32, shape, dim)


def _norm_kernel(*refs, has_add, n_pre, pre_dtypes):
    it = iter(refs)
    x_ref = next(it)
    if has_add:
        mix_ref = next(it)
        gpost_ref = next(it)
    gpre_refs = [next(it) for _ in range(n_pre)]
    x = x_ref[...]
    if has_add:
        x = x + _rms(mix_ref[...], gpost_ref[...])
        xo_ref = next(it)
        xo_ref[...] = x
    if n_pre:
        xn = x * lax.rsqrt(jnp.mean(x * x, axis=-1, keepdims=True) + RMS_EPS)
        for g_ref, dt in zip(gpre_refs, pre_dtypes):
            o_ref = next(it)
            o_ref[...] = (xn * g_ref[...]).astype(dt)


def norm_call(x, mix=None, g_post=None, pre_gains=(), pre_dtypes=()):
    M, D = x.shape
    tr = _pick(M, (256, 128, 64, 32, 16, 8))
    row = pl.BlockSpec((tr, D), lambda i: (i, 0))
    vec = pl.BlockSpec((1, D), lambda i: (0, 0))
    has_add = mix is not None
    ins, in_specs = [x], [row]
    if has_add:
        ins += [mix, g_post.reshape(1, D)]
        in_specs += [row, vec]
    for g in pre_gains:
        ins.append(g.reshape(1, D))
        in_specs.append(vec)
    out_shape, out_specs = [], []
    if has_add:
        out_shape.append(jax.ShapeDtypeStruct((M, D), F32))
        out_specs.append(row)
    for dt in pre_dtypes:
        out_shape.append(jax.ShapeDtypeStruct((M, D), dt))
        out_specs.append(row)
    return pl.pallas_call(
        functools.partial(_norm_kernel, has_add=has_add, n_pre=len(pre_gains), pre_dtypes=tuple(pre_dtypes)),
        grid=(M // tr,), in_specs=in_specs, out_specs=out_specs, out_shape=out_shape,
        compiler_params=_cparams(("parallel",)), name="rmsnorm",
    )(*ins)


def _mix_kernel(h_ref, hp_ref, mu_ref, *o_refs):
    h = h_ref[...]
    dx = hp_ref[...] - h
    for j, o_ref in enumerate(o_refs):
        o_ref[...] = (h + dx * mu_ref[j:j + 1, :]).astype(BF16)


def mix_call(h, hp, mu):
    M, D = h.shape
    n = mu.shape[0]
    tr = _pick(M, (256, 128, 64, 32, 16, 8))
    row = pl.BlockSpec((tr, D), lambda i: (i, 0))
    return pl.pallas_call(
        _mix_kernel, grid=(M // tr,),
        in_specs=[row, row, pl.BlockSpec((n, D), lambda i: (0, 0))],
        out_specs=[row] * n, out_shape=[jax.ShapeDtypeStruct((M, D), BF16)] * n,
        compiler_params=_cparams(("parallel",)), name="token_shift_mix",
    )(h, hp, mu)


def _norm_mix_kernel(x_ref, xp_ref, g_ref, mu_ref, *o_refs, seq_len):
    tr = x_ref.shape[0]
    g = g_ref[...]
    h = _rms(x_ref[...], g)
    prev_last = _rms(xp_ref[...], g)[7:8, :]
    at_seq_start = lax.rem(pl.program_id(0) * tr, seq_len) == 0
    prev_last = jnp.where(at_seq_start, 0.0, prev_last)
    hp = jnp.where(_iota2(h.shape, 0) == 0, prev_last, pltpu.roll(h, 1, 0))
    dx = hp - h
    for j, o_ref in enumerate(o_refs):
        o_ref[...] = (h + dx * mu_ref[j:j + 1, :]).astype(BF16)


def norm_mix_call(x, g, mu, *, seq_len):
    M, D = x.shape
    n = mu.shape[0]
    tr = _pick(seq_len, (256, 128, 64, 32, 16, 8))
    assert seq_len % tr == 0 and M % seq_len == 0 and tr % 8 == 0
    row = pl.BlockSpec((tr, D), lambda i: (i, 0))
    return pl.pallas_call(
        functools.partial(_norm_mix_kernel, seq_len=seq_len), grid=(M // tr,),
        in_specs=[row, pl.BlockSpec((8, D), lambda i: (jnp.maximum(i * (tr // 8) - 1, 0), 0)),
                  pl.BlockSpec((1, D), lambda i: (0, 0)), pl.BlockSpec((n, D), lambda i: (0, 0))],
        out_specs=[row] * n, out_shape=[jax.ShapeDtypeStruct((M, D), BF16)] * n,
        compiler_params=_cparams(("parallel",)), name="norm_shift_mix",
    )(x, x, g.reshape(1, D), mu)


def _matmul_kernel(a_ref, *refs, nb, ne, epilogue, emit_bf16):
    b_refs, e_refs, o_refs = refs[:nb], refs[nb:nb + ne], refs[nb + ne:]
    a = a_ref[...]
    bvals = [b[...].astype(BF16) for b in b_refs]
    accs = [jnp.dot(a, b, preferred_element_type=F32) for b in bvals]
    if emit_bf16:
        for b, o_ref in zip(bvals, o_refs[-nb:]):
            o_ref[...] = b
        o_refs = o_refs[:-nb]
    epilogue(accs, e_refs, o_refs)


def _ep_store(accs, e_refs, o_refs):
    o_refs[0][...] = accs[0].astype(o_refs[0].dtype)


def _ep_swiglu(accs, e_refs, o_refs):
    g, u = accs
    o_refs[0][...] = (g * jax.nn.sigmoid(g) * u).astype(o_refs[0].dtype)


def matmul_call(a, bs, *, tm, tn, name, epilogue=_ep_store, extras=(), outs=None, emit_bf16=False):
    M, K = a.shape
    bs = [b if isinstance(b, tuple) else (b[None], 0) for b in bs]
    N = bs[0][0].shape[2]
    tm, tn = min(tm, M), min(tn, N)
    assert M % tm == 0 and N % tn == 0, (M, N, tm, tn)
    if outs is None:
        outs = [((M, N), F32, (tm, tn), lambda i, j: (i, j))]
    elif callable(outs):
        outs = outs(tm, tn)
    if emit_bf16:
        assert M == tm, "every weight tile must be visited exactly once"
        outs = list(outs) + [((K, N), BF16, (K, tn), lambda i, j: (0, j)) for _ in bs]
    in_specs = [pl.BlockSpec((tm, K), lambda i, j: (i, 0))]
    in_specs += [pl.BlockSpec((None, K, tn), lambda i, j, l=l: (l, 0, j)) for _, l in bs]
    bs = [b for b, _ in bs]
    in_specs += [pl.BlockSpec(bshape, imap) for _, bshape, imap in extras]
    return pl.pallas_call(
        functools.partial(_matmul_kernel, nb=len(bs), ne=len(extras), epilogue=epilogue, emit_bf16=emit_bf16),
        grid=(M // tm, N // tn), in_specs=in_specs,
        out_specs=[pl.BlockSpec(bshape, imap) for _, _, bshape, imap in outs],
        out_shape=[jax.ShapeDtypeStruct(shape, dt) for shape, dt, _, _ in outs],
        compiler_params=_cparams(("parallel", "arbitrary")), name=name,
    )(a, *bs, *[e[0] for e in extras])


def _lora_kernel(a_ref, w1_ref, w2_ref, bias_ref, o_ref, *, kind):
    z = jnp.dot(a_ref[...], w1_ref[...], preferred_element_type=F32)
    if kind == "decay":
        z = jnp.tanh(z)
    elif kind == "gate":
        z = jax.nn.sigmoid(z)
    y = jnp.dot(z.astype(BF16), w2_ref[...], preferred_element_type=F32)
    if kind == "decay":
        y = -jnp.exp(-jax.nn.softplus(-(bias_ref[...] + y)) - 0.5)
    elif kind == "iclr":
        y = jax.nn.sigmoid(bias_ref[...] + y)
    o_ref[...] = y


def lora_call(a, w1, w2, bias, kind):
    M, D = a.shape
    R = w1.shape[1]
    tm = _pick(M, (512, 256, 128, 64, 32, 16, 8))
    return pl.pallas_call(
        functools.partial(_lora_kernel, kind=kind), grid=(M // tm,),
        in_specs=[pl.BlockSpec((tm, D), lambda i: (i, 0)), pl.BlockSpec((D, R), lambda i: (0, 0)),
                  pl.BlockSpec((R, D), lambda i: (0, 0)), pl.BlockSpec((1, D), lambda i: (0, 0))],
        out_specs=pl.BlockSpec((tm, D), lambda i: (i, 0)),
        out_shape=jax.ShapeDtypeStruct((M, D), F32),
        compiler_params=_cparams(("parallel",)), name="lora_" + kind,
    )(a, w1, w2, bias.reshape(1, D))


def _split2(x):
    hi = x.astype(BF16)
    lo = (x - hi.astype(F32)).astype(BF16)
    return hi, lo


def _split3(x):
    h1 = x.astype(BF16)
    r1 = x - h1.astype(F32)
    h2 = r1.astype(BF16)
    h3 = (r1 - h2.astype(F32)).astype(BF16)
    return h1, h2, h3


def _mm_exact3(a, b_bf16, dims):
    h1, h2, h3 = _split3(a)
    return _dg(h1, b_bf16, dims) + (_dg(h2, b_bf16, dims) + _dg(h3, b_bf16, dims))


def _segsum(x, ones_bd):
    return _dg(x.astype(BF16), ones_bd, _NN)


def _wkv_groups(r, k, v, lw, a, g, kkp, kap, rkp, lnw, lnb, s_bd, ones_bd, tri):
    C = WKV_CHUNK
    G = WKV_GROUP // HEAD_A

    def each(f, *ls):
        return [f(*xs) for xs in zip(*ls)]

    def bd(x):
        return jnp.concatenate([x.astype(BF16)] * G, axis=0) * ones_bd

    def mm(x, y_bd):
        return _dg(x.astype(BF16), y_bd, _NN)

    def cat(x, y):
        return jnp.concatenate([x, y], axis=0)

    kk = each(lambda k_, p_: k_ * p_, k, kkp)
    ssq = each(lambda x: _segsum(x * x, ones_bd), kk)
    kk = each(lambda x, s: x * lax.rsqrt(jnp.maximum(s, 1e-24)), kk, ssq)
    kmod = each(lambda k_, a_, p_: k_ * (1.0 + (a_ - 1.0) * p_), k, a, kap)
    b = each(lambda x, a_: x * a_, kk, a)

    def cumsum(x):
        hi, lo = _split2(x)
        return _dg(tri, hi, _NN) + _dg(tri, lo, _NN)

    L = each(cumsum, lw)
    Ltot = each(lambda x: x[C - 1:C, :], L)
    enL = each(lambda x: jnp.exp(-x), L)
    eR = each(lambda t_, x: jnp.exp(t_ - x), Ltot, L)
    at = each(lambda x, l_, w_: -x * jnp.exp(l_ - w_), kk, L, lw)
    bt = each(lambda x, e: x * e, b, enL)
    kt = each(lambda x, e: x * e, kmod, enL)
    rt = each(lambda x, l_: x * jnp.exp(l_), r, L)
    bh = each(lambda x, e: x * e, b, eR)
    kh = each(lambda x, e: x * e, kmod, eR)

    t_i = _iota2((C, WKV_GROUP), 0)
    s_i = _iota2((C, WKV_GROUP), 1) % C
    lhs2 = each(lambda x, y: cat(x, y).astype(BF16), at, rt)
    p_b = each(lambda x, y: _dg(x, bd(y), _NT), lhs2, bt)
    p_k = each(lambda x, y: _dg(x, bd(y), _NT), lhs2, kt)
    a_ab = each(lambda x: jnp.where(s_i < t_i, x[:C], 0.0), p_b)
    a_rb = each(lambda x: jnp.where(s_i <= t_i, x[C:], 0.0), p_b)
    a_ak = each(lambda x: jnp.where(s_i < t_i, x[:C], 0.0), p_k)
    a_rk = each(lambda x: jnp.where(s_i <= t_i, x[C:], 0.0), p_k)

    eye = jnp.where(s_i == t_i, 1.0, 0.0)
    tinv = each(lambda x: eye + x, a_ab)
    nk = each(lambda x: mm(x, bd(x)), a_ab)
    for _ in range(4):
        both = each(lambda n_, t_: mm(cat(n_, t_), bd(n_)), nk, tinv)
        tinv = each(lambda t_, x: t_ + x[C:], tinv, both)
        nk = each(lambda x: x[:C], both)
    tinv = each(lambda t_, n_: t_ + mm(t_, bd(n_)), tinv, nk)

    sprod = each(lambda x, s: _dg(x, s.astype(BF16), _NT), lhs2, s_bd)
    akrk = each(lambda x, y, v_: mm(cat(x, y), bd(v_)), a_ak, a_rk, v)
    u = each(lambda t_, s, x: mm(t_, bd(s[:C] + x[:C])), tinv, sprod, akrk)
    o = each(lambda s, x, m_, u_: s[C:] + x[C:] + mm(m_, bd(u_)), sprod, akrk, a_rb, u)

    upd = each(lambda u_, v_, b_, k_: _dg(cat(u_, v_).astype(BF16), cat(b_, k_).astype(BF16), _TN),
               u, v, bh, kh)
    mask_f = ones_bd.astype(F32)
    s_new = each(lambda s, t_, x: s * jnp.exp(t_) + x * mask_f, s_bd, Ltot, upd)

    inv_n = 1.0 / HEAD_A
    sums = each(lambda o_, r_, k_, p_: _segsum(cat(o_, r_ * k_ * p_), ones_bd), o, r, kmod, rkp)
    d = each(lambda o_, s: o_ - s[:C] * inv_n, o, sums)
    var = each(lambda x: _segsum(x * x, ones_bd) * inv_n, d)
    y = each(lambda d_, var_, w_, b_, s, v_, g_: (d_ * lax.rsqrt(var_ + GN_EPS) * w_ + b_ + s[C:] * v_) * g_,
             d, var, lnw, lnb, sums, v, g)
    return y, s_new


def _wkv_chunk_kernel(r_ref, k_ref, v_ref, lw_ref, a_ref, g_ref, kk_ref, ka_ref, rk_ref, lnw_ref, lnb_ref,
                      y_ref, st_ref, sbd_ref):
    C = WKV_CHUNK
    ng = r_ref.shape[1] // WKV_GROUP
    c_idx = pl.program_id(2)
    W = WKV_GROUP
    same_head = (_iota2((W, W), 0) // HEAD_A) == (_iota2((W, W), 1) // HEAD_A)
    ones_bd = jnp.where(same_head, 1.0, 0.0).astype(BF16)
    tri = jnp.where(_iota2((C, C), 1) <= _iota2((C, C), 0), 1.0, 0.0).astype(BF16)

    @pl.when(c_idx == 0)
    def _():
        sbd_ref[...] = jnp.zeros(sbd_ref.shape, F32)

    sls = [slice(W * q, W * (q + 1)) for q in range(ng)]
    split = lambda ref: [ref[:, sl] for sl in sls]
    ys, s_news = _wkv_groups(*[split(ref) for ref in (r_ref, k_ref, v_ref, lw_ref, a_ref, g_ref, kk_ref, ka_ref,
                                                      rk_ref, lnw_ref, lnb_ref)],
                             [sbd_ref[q] for q in range(ng)], ones_bd, tri)
    for q in range(ng):
        y_ref[:, sls[q]] = ys[q].astype(y_ref.dtype)
        sbd_ref[q] = s_news[q]

    @pl.when(c_idx == pl.num_programs(2) - 1)
    def _():
        fold = jnp.where((_iota2((HEAD_A, W), 1) % HEAD_A) == _iota2((HEAD_A, W), 0), 1.0, 0.0).astype(BF16)
        for q in range(ng):
            rows = _mm_exact3(sbd_ref[q], fold, _NT)
            st_ref[0, 4 * q:4 * q + 4] = rows.reshape(4, HEAD_A, HEAD_A)


def wkv_chunk_call(r, k, v, lw, a, g, k_k, k_a, r_k, ln_w, ln_b, *, T):
    M, D = r.shape
    B = M // T
    H = D // HEAD_A
    C = WKV_CHUNK
    assert C == HEAD_A and T % C == 0
    Lw = _pick(D, (8 * WKV_GROUP, 4 * WKV_GROUP, 2 * WKV_GROUP, WKV_GROUP))
    assert D % Lw == 0 and Lw % WKV_GROUP == 0
    hb = Lw // HEAD_A
    nc = T // C
    row = pl.BlockSpec((C, Lw), lambda bi, hg, c: (bi * nc + c, hg))
    vec = pl.BlockSpec((1, Lw), lambda bi, hg, c: (0, hg))
    st = pl.BlockSpec((1, hb, HEAD_A, HEAD_A), lambda bi, hg, c: (bi, hg, 0, 0))
    vecs = [x.reshape(1, D) for x in (k_k, k_a, r_k, ln_w, ln_b)]
    return pl.pallas_call(
        _wkv_chunk_kernel,
        grid=(B, D // Lw, nc),
        in_specs=[row] * 6 + [vec] * 5,
        out_specs=[row, st],
        out_shape=[jax.ShapeDtypeStruct((M, D), BF16), jax.ShapeDtypeStruct((B, H, HEAD_A, HEAD_A), F32)],
        scratch_shapes=[pltpu.VMEM((Lw // WKV_GROUP, WKV_GROUP, WKV_GROUP), F32)],
        compiler_params=_cparams(("parallel", "parallel", "arbitrary")), name="wkv_chunk",
    )(r, k, v, lw, a, g, *vecs)


def _wkv_step_kernel(r_ref, k_ref, v_ref, lw_ref, a_ref, g_ref, kk_ref, ka_ref, rk_ref, lnw_ref, lnb_ref,
                     s_ref, y_ref, so_ref, op_ref, sa_ref, o_ref, *, T, hb):
    N = HEAD_A
    for h in range(hb):
        sl = slice(h * N, (h + 1) * N)
        for t in range(T):
            k = k_ref[t, sl, :]
            a = a_ref[t, sl, :]
            kk = k * kk_ref[sl, :]
            kk = kk * lax.rsqrt(jnp.maximum(jnp.sum(kk * kk, axis=0, keepdims=True), 1e-24))
            op_ref[t, 0] = jnp.exp(lw_ref[t, sl, :])
            op_ref[t, 1] = kk * a
            op_ref[t, 2] = k * (1.0 + (a - 1.0) * ka_ref[sl, :])
            op_ref[t, 3] = r_ref[t, sl, :]
            op_ref[t, 4] = -kk

        def first(vi, carry, h=h):
            sa_ref[0, pl.ds(vi, 1), :] = jnp.sum(s_ref[h, vi] * op_ref[0, 4], axis=0, keepdims=True)
            return carry

        lax.fori_loop(0, N, first, 0, unroll=8)
        for t in range(T):
            src = s_ref if t == 0 else so_ref
            cur, nxt = t % 2, (t + 1) % 2

            def body(vi, carry, h=h, t=t, src=src, cur=cur, nxt=nxt):
                s_row = src[h, vi]
                sa_row = sa_ref[cur, pl.ds(vi, 1), :]
                v_row = v_ref[t, pl.ds(h * N + vi, 1), :]
                s_new = s_row * op_ref[t, 0] + sa_row * op_ref[t, 1] + v_row * op_ref[t, 2]
                so_ref[h, vi] = s_new
                o_ref[pl.ds(vi, 1), :] = jnp.sum(s_new * op_ref[t, 3], axis=0, keepdims=True)
                if t + 1 < T:
                    sa_ref[nxt, pl.ds(vi, 1), :] = jnp.sum(s_new * op_ref[t + 1, 4], axis=0, keepdims=True)
                return carry

            lax.fori_loop(0, N, body, 0, unroll=2)
            o = o_ref[...]
            d = o - jnp.mean(o, axis=0, keepdims=True)
            var = jnp.mean(d * d, axis=0, keepdims=True)
            o_n = d * lax.rsqrt(var + GN_EPS) * lnw_ref[sl, :] + lnb_ref[sl, :]
            bonus = jnp.sum(op_ref[t, 3] * op_ref[t, 2] * rk_ref[sl, :], axis=0, keepdims=True) * v_ref[t, sl, :]
            y_ref[t, sl, :] = ((o_n + bonus) * g_ref[t, sl, :]).astype(y_ref.dtype)


def wkv_step_call(r, k, v, lw, a, g, k_k, k_a, r_k, ln_w, ln_b, s0, *, T):
    _, D, Bd = r.shape
    H = D // HEAD_A
    hb = 2 if H % 2 == 0 else 1
    seq = pl.BlockSpec((T, hb * HEAD_A, Bd), lambda i: (0, i, 0))
    par = pl.BlockSpec((hb * HEAD_A, Bd), lambda i: (i, 0))
    st = pl.BlockSpec((hb, HEAD_A, HEAD_A, Bd), lambda i: (i, 0, 0, 0))
    pars = [jnp.broadcast_to(x.reshape(D, 1), (D, Bd)) for x in (k_k, k_a, r_k, ln_w, ln_b)]
    return pl.pallas_call(
        functools.partial(_wkv_step_kernel, T=T, hb=hb), grid=(H // hb,),
        in_specs=[seq] * 6 + [par] * 5 + [st],
        out_specs=[seq, st],
        out_shape=[jax.ShapeDtypeStruct((T, D, Bd), BF16), jax.ShapeDtypeStruct((H, HEAD_A, HEAD_A, Bd), F32)],
        scratch_shapes=[pltpu.VMEM((T, 5, HEAD_A, Bd), F32), pltpu.VMEM((2, HEAD_A, Bd), F32),
                        pltpu.VMEM((HEAD_A, Bd), F32)],
        compiler_params=_cparams(("parallel",)), name="wkv_step",
    )(r, k, v, lw, a, g, *pars, s0)


def _rope_tables(pos0, T):
    half = ROPE_DIM // 2
    inv = ROPE_THETA ** (-jnp.arange(half, dtype=F32) / half)
    ang = (pos0 + jnp.arange(T)).astype(F32)[:, None] * inv[None, :]
    cos, sin = jnp.cos(ang), jnp.sin(ang)
    return jnp.tile(cos, (1, LANES // half)), jnp.tile(sin, (1, LANES // half))


def _rot_cols(w):
    half = ROPE_DIM // 2
    return jnp.concatenate([-w[..., half:], w[..., :half]], axis=-1)


def _rope_tail(tail, cos, sin):
    rot = pltpu.roll(tail, ROPE_DIM, 1)
    return jnp.where(_iota2(tail.shape, 1) < ROPE_DIM, tail * cos + rot * sin, 0.0)


def _kv_kernel(a_ref, w_ref, g_ref, cos_ref, sin_ref, ckv_ref, kpe_ref, ckvb_ref, kpeb_ref, *, C_KV):
    kv = jnp.dot(a_ref[...], w_ref[...], preferred_element_type=F32)
    ckv = _rms(kv[:, :C_KV], g_ref[...])
    kpe = _rope_tail(kv[:, C_KV:C_KV + LANES], cos_ref[...], sin_ref[...])
    ckv_ref[...] = ckv
    kpe_ref[...] = kpe[:, :ROPE_DIM]
    ckvb_ref[...] = ckv.astype(BF16)
    kpeb_ref[...] = kpe.astype(BF16)


def kv_call(hkv, w_ext, g_a, cos, sin):
    M, D = hkv.shape
    C_KV = g_a.shape[0]
    NW = w_ext.shape[1]
    tm = _pick(M, (512, 256, 128, 64, 32, 16, 8))
    rowb = lambda n: pl.BlockSpec((tm, n), lambda i: (i, 0))
    return pl.pallas_call(
        functools.partial(_kv_kernel, C_KV=C_KV), grid=(M // tm,),
        in_specs=[rowb(D), pl.BlockSpec((D, NW), lambda i: (0, 0)), pl.BlockSpec((1, C_KV), lambda i: (0, 0)),
                  rowb(LANES), rowb(LANES)],
        out_specs=[rowb(C_KV), rowb(ROPE_DIM), rowb(C_KV), rowb(LANES)],
        out_shape=[jax.ShapeDtypeStruct((M, C_KV), F32), jax.ShapeDtypeStruct((M, ROPE_DIM), F32),
                   jax.ShapeDtypeStruct((M, C_KV), BF16), jax.ShapeDtypeStruct((M, LANES), BF16)],
        compiler_params=_cparams(("parallel",)), name="latent_kv",
    )(hkv, w_ext, g_a.reshape(1, C_KV), cos, sin)


def _ep_rmsnorm(accs, e_refs, o_refs):
    o_refs[0][...] = _rms(accs[0], e_refs[0][...]).astype(o_refs[0].dtype)


def _ep_rope_q(accs, e_refs, o_refs, *, scale):
    acc = accs[0]
    cos, sin = e_refs[0][...], e_refs[1][...]
    pieces = []
    for h in range(acc.shape[1] // 256):
        pieces += [acc[:, 256 * h:256 * h + LANES], _rope_tail(acc[:, 256 * h + LANES:256 * (h + 1)], cos, sin)]
    o_refs[0][...] = (jnp.concatenate(pieces, axis=1) * scale).astype(o_refs[0].dtype)


def _ep_expand_kv(accs, e_refs, o_refs):
    k_nope = accs[0].astype(BF16)
    pe = e_refs[0][...]
    pieces = []
    for h in range(k_nope.shape[1] // LANES):
        pieces += [k_nope[:, LANES * h:LANES * (h + 1)], pe]
    o_refs[0][...] = jnp.concatenate(pieces, axis=1)
    o_refs[1][...] = accs[1].astype(BF16)


def _attn_prompt_kernel(q_ref, k_ref, v_ref, o_ref, *, tq):
    T = q_ref.shape[0]
    causal = _iota2((tq, tq), 1) <= _iota2((tq, tq), 0)
    for qi in range(T // tq):
        n = (qi + 1) * tq
        q = q_ref[qi * tq:(qi + 1) * tq, :]
        s = _dg(q, k_ref[:n, :], _NT)
        diag = jnp.where(causal, s[:, n - tq:], NEG_INF)
        s = diag if qi == 0 else jnp.concatenate([s[:, :n - tq], diag], axis=1)
        m = jnp.max(s, axis=-1, keepdims=True)
        p = jnp.exp(s - m)
        l = jnp.sum(p, axis=-1, keepdims=True)
        o = jnp.dot(p.astype(BF16), v_ref[:n, :], preferred_element_type=F32)
        o_ref[qi * tq:(qi + 1) * tq, :] = (o / l).astype(o_ref.dtype)


def attn_prompt_call(q, k, v, *, B, T, H):
    tq = _pick(T, (512, 256, 128))
    assert T % tq == 0 and tq % LANES == 0
    return pl.pallas_call(
        functools.partial(_attn_prompt_kernel, tq=tq),
        grid=(B, H),
        in_specs=[pl.BlockSpec((T, 256), lambda b, h: (b, h)), pl.BlockSpec((T, 256), lambda b, h: (b, h)),
                  pl.BlockSpec((T, LANES), lambda b, h: (b, h))],
        out_specs=pl.BlockSpec((T, LANES), lambda b, h: (b, h)),
        out_shape=jax.ShapeDtypeStruct((B * T, H * LANES), BF16),
        compiler_params=_cparams(("parallel", "parallel")), name="attn_prompt",
    )(q, k, v)


def _absorb_q_kernel(q_ref, wk_ref, o_ref):
    q = q_ref[...]
    lat = jnp.dot(q[:, :LANES], wk_ref[0], preferred_element_type=F32)
    o_ref[...] = jnp.concatenate([lat.astype(BF16), q[:, LANES:]], axis=1)


def absorb_q_call(q, wk_t, *, H):
    M = q.shape[0]
    C_KV = wk_t.shape[2]
    return pl.pallas_call(
        _absorb_q_kernel, grid=(H,),
        in_specs=[pl.BlockSpec((M, 256), lambda h: (0, h)), pl.BlockSpec((1, LANES, C_KV), lambda h: (h, 0, 0))],
        out_specs=pl.BlockSpec((M, C_KV + LANES), lambda h: (0, h)),
        out_shape=jax.ShapeDtypeStruct((M, H * (C_KV + LANES)), BF16),
        compiler_params=_cparams(("parallel",)), name="absorb_q",
    )(q, wk_t)


def _paged_attn_kernel(pt_ref, q_ref, cn_ref, kn_ref, ckv_hbm, kpe_hbm, o_ref, ckv_buf, kpe_buf, sem,
                       m_ref, l_ref, acc_ref, *, P, C_KV, H, n_chain):
    b, j = pl.program_id(0), pl.program_id(1)
    nb, nj = pl.num_programs(0), pl.num_programs(1)
    step = b * nj + j
    slot = lax.rem(step, 2)
    is_last = step == nb * nj - 1
    wrap = j + 1 == nj
    b_nxt = jnp.where(is_last, b, jnp.where(wrap, b + 1, b))
    j_nxt = jnp.where(is_last, j, jnp.where(wrap, 0, j + 1))
    per = P // n_chain

    def page_copies(bb, jj, sl, i):
        page = pt_ref[bb, jj * P + i]
        return (pltpu.make_async_copy(ckv_hbm.at[page], ckv_buf.at[sl, i], sem.at[sl, 0]),
                pltpu.make_async_copy(kpe_hbm.at[page], kpe_buf.at[sl, i], sem.at[sl, 1]))

    def start_pages(bb, jj, sl, lo, hi):
        for i in range(lo, hi):
            for cp in page_copies(bb, jj, sl, i):
                cp.start()

    def wait_pages(bb, jj, sl):
        for i in range(P):
            for cp in page_copies(bb, jj, sl, i):
                cp.wait()

    @pl.when(step == 0)
    def _():
        start_pages(b, j, slot, 0, P)

    @pl.when(j == 0)
    def _():
        m_ref[...] = jnp.full(m_ref.shape, NEG_INF, F32)
        l_ref[...] = jnp.zeros(l_ref.shape, F32)
        acc_ref[...] = jnp.zeros(acc_ref.shape, F32)

    q = q_ref[0]
    q_lat, q_pe = q[:, :C_KV], q[:, C_KV:C_KV + ROPE_DIM]

    def update(c, s, vals):
        m_old = m_ref[c]
        m_new = jnp.maximum(m_old, jnp.max(s, axis=-1, keepdims=True))
        corr = jnp.exp(m_old - m_new)
        p = jnp.exp(s - m_new)
        l_ref[c] = l_ref[c] * corr + jnp.sum(p, axis=-1, keepdims=True)
        acc_ref[c] = acc_ref[c] * corr + jnp.dot(p.astype(BF16), vals, preferred_element_type=F32)
        m_ref[c] = m_new

    wait_pages(b, j, slot)
    kcs, scores = [], []
    for c in range(n_chain):
        start_pages(b_nxt, j_nxt, 1 - slot, c * per, (c + 1) * per)
        pages = range(c * per, (c + 1) * per)
        kc = jnp.concatenate([ckv_buf[slot, i].astype(BF16) for i in pages], axis=0)
        kr = jnp.concatenate([kpe_buf[slot, i].astype(BF16) for i in pages], axis=1)
        kcs.append(kc)
        scores.append(_dg(q_lat, kc, _NT) + _dg(q_pe, kr, _NN))
    for c in range(n_chain):
        update(c, scores[c], kcs[c])

    @pl.when(is_last)
    def _():
        wait_pages(b_nxt, j_nxt, 1 - slot)

    @pl.when(j == nj - 1)
    def _():
        cn = cn_ref[0]
        kn = kn_ref[0][:, :ROPE_DIM]
        sn = _dg(q_lat, cn, _NT) + _dg(q_pe, kn, _NT)
        sn = jnp.where(_iota2(sn.shape, 1) <= _iota2(sn.shape, 0) // H, sn, NEG_INF)
        update(0, sn, cn)
        m = m_ref[0]
        for c in range(1, n_chain):
            m = jnp.maximum(m, m_ref[c])
        acc = jnp.zeros(acc_ref.shape[1:], F32)
        l = jnp.zeros(l_ref.shape[1:], F32)
        for c in range(n_chain):
            wgt = jnp.exp(m_ref[c] - m)
            acc = acc + acc_ref[c] * wgt
            l = l + l_ref[c] * wgt
        o_ref[0] = (acc / l).astype(o_ref.dtype)


def paged_attn_call(qcat, ckv_new, kpe_new, cache_ckv, cache_kpe_t, page_table, *, H):
    Bd, R, QW = qcat.shape
    C_KV = QW - LANES
    n_pages = page_table.shape[1]
    P = _pick(n_pages, (32, 16, 8, 4, 2))
    n_chain = _pick(P, (4, 2, 1))
    page = cache_ckv.shape[1]

    in_specs = [pl.BlockSpec((1, R, QW), lambda b, j, pt: (b, 0, 0)),
                pl.BlockSpec((1, 8, C_KV), lambda b, j, pt: (b, 0, 0)),
                pl.BlockSpec((1, 8, LANES), lambda b, j, pt: (b, 0, 0)),
                pl.BlockSpec(memory_space=pl.ANY), pl.BlockSpec(memory_space=pl.ANY)]
    grid_spec = pltpu.PrefetchScalarGridSpec(
        num_scalar_prefetch=1, grid=(Bd, n_pages // P), in_specs=in_specs,
        out_specs=pl.BlockSpec((1, R, C_KV), lambda b, j, pt: (b, 0, 0)),
        scratch_shapes=[pltpu.VMEM((2, P, page, C_KV), F32), pltpu.VMEM((2, P, ROPE_DIM, page), F32),
                        pltpu.SemaphoreType.DMA((2, 2)),
                        pltpu.VMEM((n_chain, R, 1), F32), pltpu.VMEM((n_chain, R, 1), F32),
                        pltpu.VMEM((n_chain, R, C_KV), F32)])
    return pl.pallas_call(
        functools.partial(_paged_attn_kernel, P=P, C_KV=C_KV, H=H, n_chain=n_chain),
        grid_spec=grid_spec, out_shape=jax.ShapeDtypeStruct((Bd, R, C_KV), BF16),
        compiler_params=_cparams(("arbitrary", "arbitrary")), name="paged_attn",
    )(page_table, qcat, ckv_new, kpe_new, cache_ckv, cache_kpe_t)


def _unabsorb_kernel(o_ref, wv_ref, out_ref):
    out_ref[...] = jnp.dot(o_ref[...], wv_ref[0], preferred_element_type=F32).astype(out_ref.dtype)


def unabsorb_call(o_lat, wv, *, H):
    M = o_lat.shape[0]
    C_KV = wv.shape[1]
    return pl.pallas_call(
        _unabsorb_kernel, grid=(H,),
        in_specs=[pl.BlockSpec((M, C_KV), lambda h: (0, h)), pl.BlockSpec((1, C_KV, LANES), lambda h: (h, 0, 0))],
        out_specs=pl.BlockSpec((M, LANES), lambda h: (0, h)),
        out_shape=jax.ShapeDtypeStruct((M, H * LANES), BF16),
        compiler_params=_cparams(("parallel",)), name="unabsorb_o",
    )(o_lat, wv)


def _pad_cols(w, n):
    return w if w.shape[1] == n else jnp.pad(w, ((0, 0), (0, n - w.shape[1])))


def _pad_rows(w, n):
    return w if w.shape[0] == n else jnp.pad(w, ((0, n - w.shape[0]), (0, 0)))


def _round_up(n, m):
    return -(-n // m) * m


def _prep_weights(p):
    w = {"q_w_a": p["q_w_a"].astype(BF16)}
    for n1, n2 in (("rw_w1", "rw_w2"), ("rw_a1", "rw_a2"), ("rw_g1", "rw_g2")):
        R = _round_up(p[n1].shape[2], LANES)
        w[n1] = [_pad_cols(x, R).astype(BF16) for x in p[n1]]
        w[n2] = [_pad_rows(x, R).astype(BF16) for x in p[n2]]
    C_KV = p["kv_g_a"].shape[0]
    kv_w_a = p["kv_w_a"]
    w["kv_w_a"] = jnp.concatenate([kv_w_a, _rot_cols(kv_w_a[:, C_KV:])], axis=1).astype(BF16)
    H = p["kv_w_b"].shape[1]
    nope = p["kv_w_b"].shape[2] - LANES
    assert nope == LANES and H % 2 == 0 and kv_w_a.shape[1] == C_KV + ROPE_DIM
    Bn, QL, _ = p["q_w_b"].shape
    qb = p["q_w_b"].reshape(Bn, QL, H, nope + ROPE_DIM)
    qb = jnp.concatenate([qb, _rot_cols(qb[..., nope:])], axis=-1)
    w["q_w_b"] = [qb[l].reshape(QL, H * 256).astype(BF16) for l in range(Bn)]
    wk = p["kv_w_b"][..., :nope]
    wv = p["kv_w_b"][..., nope:]
    w["wk_cols"] = wk.reshape(C_KV, H * LANES).astype(BF16)
    w["wv_cols"] = wv.reshape(C_KV, H * LANES).astype(BF16)
    w["wk_t"] = jnp.transpose(wk, (1, 2, 0)).astype(BF16)
    w["wv_h"] = jnp.transpose(wv, (1, 0, 2)).astype(BF16)
    return w


class _BigWeights:
    def __init__(self, p):
        self.p = p
        self.copies = {}
        self.emitting = True

    def matmul(self, a, names, l, *, tn, tn_emit=None, **kw):
        if self.emitting:
            res = matmul_call(a, [(self.p[n], l) for n in names], tn=tn_emit or tn, emit_bf16=True, **kw)
            for n, c in zip(names, res[-len(names):]):
                self.copies[(n, l)] = c
            return res[:-len(names)]
        return matmul_call(a, [self.copies[(n, l)] for n in names], tn=tn, **kw)


def _ffn(x, mix, l, p, big, tm):
    x, h = norm_call(x, mix, p["g_mix_post"][l], [p["g_ffn_pre"][l]], [BF16])
    M = x.shape[0]
    F = p["ffn_w_gate"].shape[2]
    act = big.matmul(h, ["ffn_w_gate", "ffn_w_up"], l, tm=tm, tn=256, epilogue=_ep_swiglu, name="ffn_gate_up",
                     outs=lambda tm_, tn_: [((M, F), BF16, (tm_, tn_), lambda i, j: (i, j))])[0]
    y = big.matmul(act, ["ffn_w_down"], l, tm=min(tm, 512), tn=512, tn_emit=LANES, name="ffn_down")[0]
    return x, y


def _rwkv_layer(x, shift_prev, s0, p, w, big, *, B, T, tm, chunked):
    l = 0
    M, D = x.shape
    if chunked:
        m_r, m_w, m_k, m_v, m_a, m_g = norm_mix_call(x, p["g_mix_pre"][l], p["rw_mu"][l], seq_len=T)
        (shift_out,) = norm_call(x.reshape(B, T, D)[:, -1], pre_gains=[p["g_mix_pre"][l]], pre_dtypes=[F32])
    else:
        (h,) = norm_call(x, pre_gains=[p["g_mix_pre"][l]], pre_dtypes=[F32])
        h3 = h.reshape(B, T, D)
        hp = jnp.concatenate([shift_prev[:, None, :], h3[:, :-1]], axis=1).reshape(M, D)
        shift_out = h3[:, -1]
        m_r, m_w, m_k, m_v, m_a, m_g = mix_call(h, hp, p["rw_mu"][l])
    r = big.matmul(m_r, ["rw_w_r"], l, tm=tm, tn=512, name="rwkv_r")[0]
    k = big.matmul(m_k, ["rw_w_k"], l, tm=tm, tn=512, name="rwkv_k")[0]
    v = big.matmul(m_v, ["rw_w_v"], l, tm=tm, tn=512, name="rwkv_v")[0]
    lw = lora_call(m_w, w["rw_w1"][l], w["rw_w2"][l], p["rw_w0"][l], "decay")
    a = lora_call(m_a, w["rw_a1"][l], w["rw_a2"][l], p["rw_a0"][l], "iclr")
    g = lora_call(m_g, w["rw_g1"][l], w["rw_g2"][l], jnp.zeros((D,), F32), "gate")
    pars = (p["rw_k_k"][l], p["rw_k_a"][l], p["rw_r_k"][l].reshape(D), p["rw_ln_w"][l], p["rw_ln_b"][l])
    if chunked:
        yg, s_new = wkv_chunk_call(r, k, v, lw, a, g, *pars, T=T)
    else:
        tmaj = lambda z: jnp.transpose(z.reshape(B, T, D), (1, 2, 0))
        yg_t, s_t = wkv_step_call(*[tmaj(z) for z in (r, k, v, lw, a, g)], *pars,
                                  jnp.transpose(s0, (1, 2, 3, 0)), T=T)
        yg = jnp.transpose(yg_t, (2, 0, 1)).reshape(M, D)
        s_new = jnp.transpose(s_t, (3, 0, 1, 2))
    mix = big.matmul(yg, ["rw_w_o"], l, tm=tm, tn=512, name="rwkv_o")[0]
    return mix, s_new, shift_out


def _mla_common(x_h1, hkv, p, w, j, *, pos0, B, T, tm, scale):
    M = x_h1.shape[0]
    cos, sin = _rope_tables(pos0, T)
    cos = jnp.tile(cos, (B, 1))
    sin = jnp.tile(sin, (B, 1))
    ckv, kpe, ckv_b, kpe_b = kv_call(hkv, w["kv_w_a"], p["kv_g_a"], cos, sin)
    QL = w["q_w_a"].shape[2]
    tma = min(tm, 512, M)
    qa = matmul_call(x_h1, [(w["q_w_a"], j)], tm=tma, tn=QL, epilogue=_ep_rmsnorm, name="q_down",
                     extras=[(p["q_g_a"][j].reshape(1, QL), (1, QL), lambda i, jj: (0, 0))],
                     outs=[((M, QL), BF16, (tma, QL), lambda i, jj: (i, 0))])[0]
    HN = w["q_w_b"][j].shape[1]
    tmq = min(tm, M)
    tnq = _pick(HN, (1024, 512, 256))
    q = matmul_call(qa, [w["q_w_b"][j]], tm=tmq, tn=tnq, epilogue=functools.partial(_ep_rope_q, scale=scale),
                    name="q_up_rope",
                    extras=[(cos, (tmq, LANES), lambda i, jj: (i, 0)), (sin, (tmq, LANES), lambda i, jj: (i, 0))],
                    outs=[((M, HN), BF16, (tmq, tnq), lambda i, jj: (i, jj))])[0]
    return ckv, kpe, ckv_b, kpe_b, q


def kernel(x_prompt, x_sample, state_wkv, state_shift, cache_ckv, cache_kpe, page_table, g_mix_pre, g_mix_post,
           g_ffn_pre, g_ffn_post, rw_mu, rw_w_r, rw_w_k, rw_w_v, rw_w_o, rw_w0, rw_w1, rw_w2, rw_a0, rw_a1, rw_a2,
           rw_g1, rw_g2, rw_k_k, rw_k_a, rw_r_k, rw_ln_w, rw_ln_b, kv_g_in, kv_w_a, kv_g_a, kv_w_b, q_w_a, q_g_a,
           q_w_b, mla_w_o, ffn_w_gate, ffn_w_up, ffn_w_down):
    p = dict(g_mix_pre=g_mix_pre, g_mix_post=g_mix_post, g_ffn_pre=g_ffn_pre, g_ffn_post=g_ffn_post,
             rw_mu=rw_mu, rw_w_r=rw_w_r, rw_w_k=rw_w_k, rw_w_v=rw_w_v, rw_w_o=rw_w_o,
             rw_w0=rw_w0, rw_w1=rw_w1, rw_w2=rw_w2, rw_a0=rw_a0, rw_a1=rw_a1, rw_a2=rw_a2,
             rw_g1=rw_g1, rw_g2=rw_g2, rw_k_k=rw_k_k, rw_k_a=rw_k_a, rw_r_k=rw_r_k,
             rw_ln_w=rw_ln_w, rw_ln_b=rw_ln_b, kv_g_in=kv_g_in, kv_w_a=kv_w_a, kv_g_a=kv_g_a, kv_w_b=kv_w_b,
             q_w_a=q_w_a, q_g_a=q_g_a, q_w_b=q_w_b, mla_w_o=mla_w_o,
             ffn_w_gate=ffn_w_gate, ffn_w_up=ffn_w_up, ffn_w_down=ffn_w_down)
    assert g_mix_pre.shape[0] == 2 and rw_mu.shape[0] == 1 and q_w_a.shape[0] == 1, "one RWKV-7 + one MLA layer"
    assert cache_ckv.shape[1] == PAGE_SIZE
    w = _prep_weights(p)
    D = x_prompt.shape[-1]
    H = kv_w_b.shape[1]
    C_KV = kv_g_a.shape[0]
    scale = float((LANES + ROPE_DIM) ** -0.5)
    outs = {}
    big = _BigWeights(p)
    for grp in ("sample", "prompt"):
        big.emitting = grp == "sample"
        if grp == "prompt":
            x3 = x_prompt
            B, T, _ = x3.shape
            s0 = None
            shift0 = jnp.zeros((B, D), F32)
            pos0 = 0
        else:
            x3 = x_sample
            B, T, _ = x3.shape
            s0 = state_wkv[0]
            shift0 = state_shift[0]
            pos0 = page_table.shape[1] * PAGE_SIZE
        M = B * T
        tm = _pick(M, (1024, 512, 256, 128, 64, 32, 16, 8))
        x = x3.reshape(M, D)
        mix, s_new, shift_out = _rwkv_layer(x, shift0, s0, p, w, big, B=B, T=T, tm=tm, chunked=(grp == "prompt"))
        x, y = _ffn(x, mix, 0, p, big, tm)
        x, h1, hkv = norm_call(x, y, g_ffn_post[0], [g_mix_pre[1], kv_g_in], [BF16, BF16])
        ckv, kpe, ckv_b, kpe_b, q = _mla_common(h1, hkv, p, w, 0, pos0=pos0, B=B, T=T, tm=tm, scale=scale)
        if grp == "prompt":
            tme = min(tm, M)
            tne = _pick(H * LANES, (1024, 512, 256))
            k_all, v_all = matmul_call(
                ckv_b, [w["wk_cols"], w["wv_cols"]], tm=tme, tn=tne, epilogue=_ep_expand_kv, name="kv_expand",
                extras=[(kpe_b, (tme, LANES), lambda i, jj: (i, 0))],
                outs=[((M, H * 256), BF16, (tme, 2 * tne), lambda i, jj: (i, jj)),
                      ((M, H * LANES), BF16, (tme, tne), lambda i, jj: (i, jj))])
            o = attn_prompt_call(q, k_all, v_all, B=B, T=T, H=H)
        else:
            qcat = absorb_q_call(q, w["wk_t"], H=H).reshape(B, T * H, C_KV + LANES)
            pad8 = lambda z: jnp.pad(z.reshape(B, T, z.shape[-1]), ((0, 0), (0, 8 - T), (0, 0)))
            o_lat = paged_attn_call(qcat, pad8(ckv_b), pad8(kpe_b), cache_ckv, jnp.transpose(cache_kpe, (0, 2, 1)),
                                    page_table, H=H)
            o = unabsorb_call(o_lat.reshape(M, H * C_KV), w["wv_h"], H=H)
        mix = big.matmul(o, ["mla_w_o"], 0, tm=min(tm, 512), tn=512, tn_emit=256, name="mla_out")[0]
        x, y = _ffn(x, mix, 1, p, big, tm)
        (x,) = norm_call(x, y, g_ffn_post[1])
        outs[grp] = (x.reshape(B, T, D), s_new[None], shift_out[None], ckv.reshape(B, T, C_KV),
                     kpe.reshape(B, T, ROPE_DIM))
    yp, wp, sp, cp, kp = outs["prompt"]
    ys, ws, ss, cs, ks = outs["sample"]
    return (yp, ys, wp, sp, cp, kp, ws, ss, cs, ks)
```

```python
import functools
import math

import jax
import jax.numpy as jnp
from jax import lax
from jax.experimental import pallas as pl
from jax.experimental.pallas import tpu as pltpu

F32 = jnp.float32
BF16 = jnp.bfloat16

V7X_VMEM_BYTES = 64 * 1024 * 1024
LANES = 128
VMEM_LIMIT = V7X_VMEM_BYTES - 8 * 1024 * 1024

HEAD_A = 64
WKV_CHUNK = 64
WKV_GROUP = 256
GN_EPS = 64e-5
RMS_EPS = 1e-6
ROPE_THETA = 10000.0
ROPE_DIM = 64
PAGE_SIZE = 128
NEG_INF = float("-inf")


def _cparams(sem):
    return pltpu.CompilerParams(dimension_semantics=sem, vmem_limit_bytes=VMEM_LIMIT)


def _pick(n, cands):
    for c in cands:
        if n % c == 0:
            return c
    return n


def _rms(x, g):
    return x * lax.rsqrt(jnp.mean(x * x, axis=-1, keepdims=True) + RMS_EPS) * g


_NN = (((1,), (0,)), ((), ()))
_NT = (((1,), (1,)), ((), ()))
_TN = (((0,), (0,)), ((), ()))


def _dg(a, b, dims):
    return lax.dot_general(a, b, dims, preferred_element_type=F32)


def _iota2(shape, dim):
    return lax.broadcasted_iota(jnp.int32, shape, dim)


def _norm_kernel(*refs, has_add, n_pre, pre_dtypes):
    it = iter(refs)
    x_ref = next(it)
    if has_add:
        mix_ref = next(it)
        gpost_ref = next(it)
    gpre_refs = [next(it) for _ in range(n_pre)]
    x = x_ref[...]
    if has_add:
        x = x + _rms(mix_ref[...], gpost_ref[...])
        xo_ref = next(it)
        xo_ref[...] = x
    if n_pre:
        xn = x * lax.rsqrt(jnp.mean(x * x, axis=-1, keepdims=True) + RMS_EPS)
        for g_ref, dt in zip(gpre_refs, pre_dtypes):
            o_ref = next(it)
            o_ref[...] = (xn * g_ref[...]).astype(dt)


def norm_call(x, mix=None, g_post=None, pre_gains=(), pre_dtypes=()):
    M, D = x.shape
    tr = _pick(M, (256, 128, 64, 32, 16, 8))
    row = pl.BlockSpec((tr, D), lambda i: (i, 0))
    vec = pl.BlockSpec((1, D), lambda i: (0, 0))
    has_add = mix is not None
    ins, in_specs = [x], [row]
    if has_add:
        ins += [mix, g_post.reshape(1, D)]
        in_specs += [row, vec]
    for g in pre_gains:
        ins.append(g.reshape(1, D))
        in_specs.append(vec)
    out_shape, out_specs = [], []
    if has_add:
        out_shape.append(jax.ShapeDtypeStruct((M, D), F32))
        out_specs.append(row)
    for dt in pre_dtypes:
        out_shape.append(jax.ShapeDtypeStruct((M, D), dt))
        out_specs.append(row)
    return pl.pallas_call(
        functools.partial(_norm_kernel, has_add=has_add, n_pre=len(pre_gains), pre_dtypes=tuple(pre_dtypes)),
        grid=(M // tr,), in_specs=in_specs, out_specs=out_specs, out_shape=out_shape,
        compiler_params=_cparams(("parallel",)), name="rmsnorm",
    )(*ins)


def _mix_kernel(h_ref, hp_ref, mu_ref, *o_refs):
    h = h_ref[...]
    dx = hp_ref[...] - h
    for j, o_ref in enumerate(o_refs):
        o_ref[...] = (h + dx * mu_ref[j:j + 1, :]).astype(BF16)


def mix_call(h, hp, mu):
    M, D = h.shape
    n = mu.shape[0]
    tr = _pick(M, (256, 128, 64, 32, 16, 8))
    row = pl.BlockSpec((tr, D), lambda i: (i, 0))
    return pl.pallas_call(
        _mix_kernel, grid=(M // tr,),
        in_specs=[row, row, pl.BlockSpec((n, D), lambda i: (0, 0))],
        out_specs=[row] * n, out_shape=[jax.ShapeDtypeStruct((M, D), BF16)] * n,
        compiler_params=_cparams(("parallel",)), name="token_shift_mix",
    )(h, hp, mu)


def _norm_mix_kernel(x_ref, xp_ref, g_ref, mu_ref, *o_refs, seq_len):
    tr = x_ref.shape[0]
    g = g_ref[...]
    h = _rms(x_ref[...], g)
    prev_last = _rms(xp_ref[...], g)[7:8, :]
    at_seq_start = lax.rem(pl.program_id(0) * tr, seq_len) == 0
    prev_last = jnp.where(at_seq_start, 0.0, prev_last)
    hp = jnp.where(_iota2(h.shape, 0) == 0, prev_last, pltpu.roll(h, 1, 0))
    dx = hp - h
    for j, o_ref in enumerate(o_refs):
        o_ref[...] = (h + dx * mu_ref[j:j + 1, :]).astype(BF16)


def norm_mix_call(x, g, mu, *, seq_len):
    M, D = x.shape
    n = mu.shape[0]
    tr = _pick(seq_len, (256, 128, 64, 32, 16, 8))
    assert seq_len % tr == 0 and M % seq_len == 0 and tr % 8 == 0
    row = pl.BlockSpec((tr, D), lambda i: (i, 0))
    return pl.pallas_call(
        functools.partial(_norm_mix_kernel, seq_len=seq_len), grid=(M // tr,),
        in_specs=[row, pl.BlockSpec((8, D), lambda i: (jnp.maximum(i * (tr // 8) - 1, 0), 0)),
                  pl.BlockSpec((1, D), lambda i: (0, 0)), pl.BlockSpec((n, D), lambda i: (0, 0))],
        out_specs=[row] * n, out_shape=[jax.ShapeDtypeStruct((M, D), BF16)] * n,
        compiler_params=_cparams(("parallel",)), name="norm_shift_mix",
    )(x, x, g.reshape(1, D), mu)


def _matmul_kernel(a_ref, *refs, nb, ne, epilogue, emit_bf16):
    b_refs, e_refs, o_refs = refs[:nb], refs[nb:nb + ne], refs[nb + ne:]
    a = a_ref[...]
    bvals = [b[...].astype(BF16) for b in b_refs]
    accs = [jnp.dot(a, b, preferred_element_type=F32) for b in bvals]
    if emit_bf16:
        for b, o_ref in zip(bvals, o_refs[-nb:]):
            o_ref[...] = b
        o_refs = o_refs[:-nb]
    epilogue(accs, e_refs, o_refs)


def _ep_store(accs, e_refs, o_refs):
    o_refs[0][...] = accs[0].astype(o_refs[0].dtype)


def _ep_swiglu(accs, e_refs, o_refs):
    g, u = accs
    o_refs[0][...] = (g * jax.nn.sigmoid(g) * u).astype(o_refs[0].dtype)


def matmul_call(a, bs, *, tm, tn, name, epilogue=_ep_store, extras=(), outs=None, emit_bf16=False):
    M, K = a.shape
    bs = [b if isinstance(b, tuple) else (b[None], 0) for b in bs]
    N = bs[0][0].shape[2]
    tm, tn = min(tm, M), min(tn, N)
    assert M % tm == 0 and N % tn == 0, (M, N, tm, tn)
    if outs is None:
        outs = [((M, N), F32, (tm, tn), lambda i, j: (i, j))]
    elif callable(outs):
        outs = outs(tm, tn)
    if emit_bf16:
        assert M == tm, "every weight tile must be visited exactly once"
        outs = list(outs) + [((K, N), BF16, (K, tn), lambda i, j: (0, j)) for _ in bs]
    in_specs = [pl.BlockSpec((tm, K), lambda i, j: (i, 0))]
    in_specs += [pl.BlockSpec((None, K, tn), lambda i, j, l=l: (l, 0, j)) for _, l in bs]
    bs = [b for b, _ in bs]
    in_specs += [pl.BlockSpec(bshape, imap) for _, bshape, imap in extras]
    return pl.pallas_call(
        functools.partial(_matmul_kernel, nb=len(bs), ne=len(extras), epilogue=epilogue, emit_bf16=emit_bf16),
        grid=(M // tm, N // tn), in_specs=in_specs,
        out_specs=[pl.BlockSpec(bshape, imap) for _, _, bshape, imap in outs],
        out_shape=[jax.ShapeDtypeStruct(shape, dt) for shape, dt, _, _ in outs],
        compiler_params=_cparams(("parallel", "arbitrary")), name=name,
    )(a, *bs, *[e[0] for e in extras])


def _lora_kernel(a_ref, w1_ref, w2_ref, bias_ref, o_ref, *, kind):
    z = jnp.dot(a_ref[...], w1_ref[...], preferred_element_type=F32)
    if kind == "decay":
        z = jnp.tanh(z)
    elif kind == "gate":
        z = jax.nn.sigmoid(z)
    y = jnp.dot(z.astype(BF16), w2_ref[...], preferred_element_type=F32)
    if kind == "decay":
        y = jax.nn.sigmoid(bias_ref[...] + y) * (-math.exp(-0.5))
    elif kind == "iclr":
        y = jax.nn.sigmoid(bias_ref[...] + y)
    o_ref[...] = y


def lora_call(a, w1, w2, bias, kind):
    M, D = a.shape
    R = w1.shape[1]
    tm = _pick(M, (512, 256, 128, 64, 32, 16, 8))
    return pl.pallas_call(
        functools.partial(_lora_kernel, kind=kind), grid=(M // tm,),
        in_specs=[pl.BlockSpec((tm, D), lambda i: (i, 0)), pl.BlockSpec((D, R), lambda i: (0, 0)),
                  pl.BlockSpec((R, D), lambda i: (0, 0)), pl.BlockSpec((1, D), lambda i: (0, 0))],
        out_specs=pl.BlockSpec((tm, D), lambda i: (i, 0)),
        out_shape=jax.ShapeDtypeStruct((M, D), F32),
        compiler_params=_cparams(("parallel",)), name="lora_" + kind,
    )(a, w1, w2, bias.reshape(1, D))


def _split2(x):
    hi = x.astype(BF16)
    lo = (x - hi.astype(F32)).astype(BF16)
    return hi, lo


def _split3(x):
    h1 = x.astype(BF16)
    r1 = x - h1.astype(F32)
    h2 = r1.astype(BF16)
    h3 = (r1 - h2.astype(F32)).astype(BF16)
    return h1, h2, h3


def _mm_exact3(a, b_bf16, dims):
    h1, h2, h3 = _split3(a)
    return _dg(h1, b_bf16, dims) + (_dg(h2, b_bf16, dims) + _dg(h3, b_bf16, dims))


def _segsum(x, ones_bd):
    return _dg(x.astype(BF16), ones_bd, _NN)


def _wkv_groups(r, k, v, lw, a, g, kkp, kap, rkp, lnw, lnb, s_bd, ones_bd, tri):
    C = WKV_CHUNK
    G = WKV_GROUP // HEAD_A

    def each(f, *ls):
        return [f(*xs) for xs in zip(*ls)]

    def bd(x):
        return jnp.concatenate([x.astype(BF16)] * G, axis=0) * ones_bd

    def mm(x, y_bd):
        return _dg(x.astype(BF16), y_bd, _NN)

    def cat(x, y):
        return jnp.concatenate([x, y], axis=0)

    kk = each(lambda k_, p_: k_ * p_, k, kkp)
    ssq = each(lambda x: _segsum(x * x, ones_bd), kk)
    kk = each(lambda x, s: x * lax.rsqrt(jnp.maximum(s, 1e-24)), kk, ssq)
    kmod = each(lambda k_, a_, p_: k_ * (1.0 + (a_ - 1.0) * p_), k, a, kap)
    b = each(lambda x, a_: x * a_, kk, a)

    def cumsum(x):
        hi, lo = _split2(x)
        return _dg(tri, hi, _NN) + _dg(tri, lo, _NN)

    L = each(cumsum, lw)
    Ltot = each(lambda x: x[C - 1:C, :], L)
    enL = each(lambda x: jnp.exp(-x), L)
    eR = each(lambda t_, x: jnp.exp(t_ - x), Ltot, L)
    at = each(lambda x, l_, w_: -x * jnp.exp(l_ - w_), kk, L, lw)
    bt = each(lambda x, e: x * e, b, enL)
    kt = each(lambda x, e: x * e, kmod, enL)
    rt = each(lambda x, l_: x * jnp.exp(l_), r, L)
    bh = each(lambda x, e: x * e, b, eR)
    kh = each(lambda x, e: x * e, kmod, eR)

    t_i = _iota2((C, WKV_GROUP), 0)
    s_i = _iota2((C, WKV_GROUP), 1) % C
    lhs2 = each(lambda x, y: cat(x, y).astype(BF16), at, rt)
    p_b = each(lambda x, y: _dg(x, bd(y), _NT), lhs2, bt)
    p_k = each(lambda x, y: _dg(x, bd(y), _NT), lhs2, kt)
    a_ab = each(lambda x: jnp.where(s_i < t_i, x[:C], 0.0), p_b)
    a_rb = each(lambda x: jnp.where(s_i <= t_i, x[C:], 0.0), p_b)
    a_ak = each(lambda x: jnp.where(s_i < t_i, x[:C], 0.0), p_k)
    a_rk = each(lambda x: jnp.where(s_i <= t_i, x[C:], 0.0), p_k)

    eye = jnp.where(s_i == t_i, 1.0, 0.0)
    tinv = each(lambda x: eye + x, a_ab)
    nk = each(lambda x: mm(x, bd(x)), a_ab)
    for _ in range(4):
        both = each(lambda n_, t_: mm(cat(n_, t_), bd(n_)), nk, tinv)
        tinv = each(lambda t_, x: t_ + x[C:], tinv, both)
        nk = each(lambda x: x[:C], both)
    tinv = each(lambda t_, n_: t_ + mm(t_, bd(n_)), tinv, nk)

    sprod = each(lambda x, s: _dg(x, s.astype(BF16), _NT), lhs2, s_bd)
    akrk = each(lambda x, y, v_: mm(cat(x, y), bd(v_)), a_ak, a_rk, v)
    u = each(lambda t_, s, x: mm(t_, bd(s[:C] + x[:C])), tinv, sprod, akrk)
    o = each(lambda s, x, m_, u_: s[C:] + x[C:] + mm(m_, bd(u_)), sprod, akrk, a_rb, u)

    upd = each(lambda u_, v_, b_, k_: _dg(cat(u_, v_).astype(BF16), cat(b_, k_).astype(BF16), _TN),
               u, v, bh, kh)
    mask_f = ones_bd.astype(F32)
    s_new = each(lambda s, t_, x: s * jnp.exp(t_) + x * mask_f, s_bd, Ltot, upd)

    inv_n = 1.0 / HEAD_A
    sums = each(lambda o_, r_, k_, p_: _segsum(cat(o_, r_ * k_ * p_), ones_bd), o, r, kmod, rkp)
    d = each(lambda o_, s: o_ - s[:C] * inv_n, o, sums)
    var = each(lambda x: _segsum(x * x, ones_bd) * inv_n, d)
    y = each(lambda d_, var_, w_, b_, s, v_, g_: (d_ * lax.rsqrt(var_ + GN_EPS) * w_ + b_ + s[C:] * v_) * g_,
             d, var, lnw, lnb, sums, v, g)
    return y, s_new


def _wkv_chunk_kernel(r_ref, k_ref, v_ref, lw_ref, a_ref, g_ref, kk_ref, ka_ref, rk_ref, lnw_ref, lnb_ref,
                      y_ref, st_ref, sbd_ref):
    C = WKV_CHUNK
    ng = r_ref.shape[1] // WKV_GROUP
    c_idx = pl.program_id(2)
    W = WKV_GROUP
    same_head = (_iota2((W, W), 0) // HEAD_A) == (_iota2((W, W), 1) // HEAD_A)
    ones_bd = jnp.where(same_head, 1.0, 0.0).astype(BF16)
    tri = jnp.where(_iota2((C, C), 1) <= _iota2((C, C), 0), 1.0, 0.0).astype(BF16)

    @pl.when(c_idx == 0)
    def _():
        sbd_ref[...] = jnp.zeros(sbd_ref.shape, F32)

    sls = [slice(W * q, W * (q + 1)) for q in range(ng)]
    split = lambda ref: [ref[:, sl] for sl in sls]
    ys, s_news = _wkv_groups(*[split(ref) for ref in (r_ref, k_ref, v_ref, lw_ref, a_ref, g_ref, kk_ref, ka_ref,
                                                      rk_ref, lnw_ref, lnb_ref)],
                             [sbd_ref[q] for q in range(ng)], ones_bd, tri)
    for q in range(ng):
        y_ref[:, sls[q]] = ys[q].astype(y_ref.dtype)
        sbd_ref[q] = s_news[q]

    @pl.when(c_idx == pl.num_programs(2) - 1)
    def _():
        fold = jnp.where((_iota2((HEAD_A, W), 1) % HEAD_A) == _iota2((HEAD_A, W), 0), 1.0, 0.0).astype(BF16)
        for q in range(ng):
            rows = _mm_exact3(sbd_ref[q], fold, _NT)
            st_ref[0, 4 * q:4 * q + 4] = rows.reshape(4, HEAD_A, HEAD_A)


def wkv_chunk_call(r, k, v, lw, a, g, k_k, k_a, r_k, ln_w, ln_b, *, T):
    M, D = r.shape
    B = M // T
    H = D // HEAD_A
    C = WKV_CHUNK
    assert C == HEAD_A and T % C == 0
    Lw = _pick(D, (8 * WKV_GROUP, 4 * WKV_GROUP, 2 * WKV_GROUP, WKV_GROUP))
    assert D % Lw == 0 and Lw % WKV_GROUP == 0
    hb = Lw // HEAD_A
    nc = T // C
    row = pl.BlockSpec((C, Lw), lambda bi, hg, c: (bi * nc + c, hg))
    vec = pl.BlockSpec((1, Lw), lambda bi, hg, c: (0, hg))
    st = pl.BlockSpec((1, hb, HEAD_A, HEAD_A), lambda bi, hg, c: (bi, hg, 0, 0))
    vecs = [x.reshape(1, D) for x in (k_k, k_a, r_k, ln_w, ln_b)]
    return pl.pallas_call(
        _wkv_chunk_kernel,
        grid=(B, D // Lw, nc),
        in_specs=[row] * 6 + [vec] * 5,
        out_specs=[row, st],
        out_shape=[jax.ShapeDtypeStruct((M, D), BF16), jax.ShapeDtypeStruct((B, H, HEAD_A, HEAD_A), F32)],
        scratch_shapes=[pltpu.VMEM((Lw // WKV_GROUP, WKV_GROUP, WKV_GROUP), F32)],
        compiler_params=_cparams(("parallel", "parallel", "arbitrary")), name="wkv_chunk",
    )(r, k, v, lw, a, g, *vecs)


def _wkv_step_kernel(r_ref, k_ref, v_ref, lw_ref, a_ref, g_ref, kk_ref, ka_ref, rk_ref, lnw_ref, lnb_ref,
                     s_ref, y_ref, so_ref, op_ref, sa_ref, o_ref, *, T, hb):
    N = HEAD_A
    for h in range(hb):
        sl = slice(h * N, (h + 1) * N)
        for t in range(T):
            k = k_ref[t, sl, :]
            a = a_ref[t, sl, :]
            kk = k * kk_ref[sl, :]
            kk = kk * lax.rsqrt(jnp.maximum(jnp.sum(kk * kk, axis=0, keepdims=True), 1e-24))
            op_ref[t, 0] = jnp.exp(lw_ref[t, sl, :])
            op_ref[t, 1] = kk * a
            op_ref[t, 2] = k * (1.0 + (a - 1.0) * ka_ref[sl, :])
            op_ref[t, 3] = r_ref[t, sl, :]
            op_ref[t, 4] = -kk

        def first(vi, carry, h=h):
            sa_ref[0, pl.ds(vi, 1), :] = jnp.sum(s_ref[h, vi] * op_ref[0, 4], axis=0, keepdims=True)
            return carry

        lax.fori_loop(0, N, first, 0, unroll=8)
        for t in range(T):
            src = s_ref if t == 0 else so_ref
            cur, nxt = t % 2, (t + 1) % 2

            def body(vi, carry, h=h, t=t, src=src, cur=cur, nxt=nxt):
                s_row = src[h, vi]
                sa_row = sa_ref[cur, pl.ds(vi, 1), :]
                v_row = v_ref[t, pl.ds(h * N + vi, 1), :]
                s_new = s_row * op_ref[t, 0] + sa_row * op_ref[t, 1] + v_row * op_ref[t, 2]
                so_ref[h, vi] = s_new
                o_ref[pl.ds(vi, 1), :] = jnp.sum(s_new * op_ref[t, 3], axis=0, keepdims=True)
                if t + 1 < T:
                    sa_ref[nxt, pl.ds(vi, 1), :] = jnp.sum(s_new * op_ref[t + 1, 4], axis=0, keepdims=True)
                return carry

            lax.fori_loop(0, N, body, 0, unroll=2)
            o = o_ref[...]
            d = o - jnp.mean(o, axis=0, keepdims=True)
            var = jnp.mean(d * d, axis=0, keepdims=True)
            o_n = d * lax.rsqrt(var + GN_EPS) * lnw_ref[sl, :] + lnb_ref[sl, :]
            bonus = jnp.sum(op_ref[t, 3] * op_ref[t, 2] * rk_ref[sl, :], axis=0, keepdims=True) * v_ref[t, sl, :]
            y_ref[t, sl, :] = ((o_n + bonus) * g_ref[t, sl, :]).astype(y_ref.dtype)


def wkv_step_call(r, k, v, lw, a, g, k_k, k_a, r_k, ln_w, ln_b, s0, *, T):
    _, D, Bd = r.shape
    H = D // HEAD_A
    hb = 2 if H % 2 == 0 else 1
    seq = pl.BlockSpec((T, hb * HEAD_A, Bd), lambda i: (0, i, 0))
    par = pl.BlockSpec((hb * HEAD_A, Bd), lambda i: (i, 0))
    st = pl.BlockSpec((hb, HEAD_A, HEAD_A, Bd), lambda i: (i, 0, 0, 0))
    pars = [jnp.broadcast_to(x.reshape(D, 1), (D, Bd)) for x in (k_k, k_a, r_k, ln_w, ln_b)]
    return pl.pallas_call(
        functools.partial(_wkv_step_kernel, T=T, hb=hb), grid=(H // hb,),
        in_specs=[seq] * 6 + [par] * 5 + [st],
        out_specs=[seq, st],
        out_shape=[jax.ShapeDtypeStruct((T, D, Bd), BF16), jax.ShapeDtypeStruct((H, HEAD_A, HEAD_A, Bd), F32)],
        scratch_shapes=[pltpu.VMEM((T, 5, HEAD_A, Bd), F32), pltpu.VMEM((2, HEAD_A, Bd), F32),
                        pltpu.VMEM((HEAD_A, Bd), F32)],
        compiler_params=_cparams(("parallel",)), name="wkv_step",
    )(r, k, v, lw, a, g, *pars, s0)


def _rope_tables(pos0, T):
    half = ROPE_DIM // 2
    inv = ROPE_THETA ** (-jnp.arange(half, dtype=F32) / half)
    ang = (pos0 + jnp.arange(T)).astype(F32)[:, None] * inv[None, :]
    cos, sin = jnp.cos(ang), jnp.sin(ang)
    return jnp.tile(cos, (1, LANES // half)), jnp.tile(sin, (1, LANES // half))


def _rot_cols(w):
    half = ROPE_DIM // 2
    return jnp.concatenate([-w[..., half:], w[..., :half]], axis=-1)


def _rope_tail(tail, cos, sin):
    rot = pltpu.roll(tail, ROPE_DIM, 1)
    return jnp.where(_iota2(tail.shape, 1) < ROPE_DIM, tail * cos + rot * sin, 0.0)


def _kv_kernel(a_ref, w_ref, g_ref, cos_ref, sin_ref, ckv_ref, kpe_ref, ckvb_ref, kpeb_ref, *, C_KV):
    kv = jnp.dot(a_ref[...], w_ref[...], preferred_element_type=F32)
    ckv = _rms(kv[:, :C_KV], g_ref[...])
    kpe = _rope_tail(kv[:, C_KV:C_KV + LANES], cos_ref[...], sin_ref[...])
    ckv_ref[...] = ckv
    kpe_ref[...] = kpe[:, :ROPE_DIM]
    ckvb_ref[...] = ckv.astype(BF16)
    kpeb_ref[...] = kpe.astype(BF16)


def kv_call(hkv, w_ext, g_a, cos, sin):
    M, D = hkv.shape
    C_KV = g_a.shape[0]
    NW = w_ext.shape[1]
    tm = _pick(M, (512, 256, 128, 64, 32, 16, 8))
    rowb = lambda n: pl.BlockSpec((tm, n), lambda i: (i, 0))
    return pl.pallas_call(
        functools.partial(_kv_kernel, C_KV=C_KV), grid=(M // tm,),
        in_specs=[rowb(D), pl.BlockSpec((D, NW), lambda i: (0, 0)), pl.BlockSpec((1, C_KV), lambda i: (0, 0)),
                  rowb(LANES), rowb(LANES)],
        out_specs=[rowb(C_KV), rowb(ROPE_DIM), rowb(C_KV), rowb(LANES)],
        out_shape=[jax.ShapeDtypeStruct((M, C_KV), F32), jax.ShapeDtypeStruct((M, ROPE_DIM), F32),
                   jax.ShapeDtypeStruct((M, C_KV), BF16), jax.ShapeDtypeStruct((M, LANES), BF16)],
        compiler_params=_cparams(("parallel",)), name="latent_kv",
    )(hkv, w_ext, g_a.reshape(1, C_KV), cos, sin)


def _ep_rmsnorm(accs, e_refs, o_refs):
    o_refs[0][...] = _rms(accs[0], e_refs[0][...]).astype(o_refs[0].dtype)


def _ep_rope_q(accs, e_refs, o_refs, *, scale):
    acc = accs[0]
    cos, sin = e_refs[0][...], e_refs[1][...]
    pieces = []
    for h in range(acc.shape[1] // 256):
        pieces += [acc[:, 256 * h:256 * h + LANES], _rope_tail(acc[:, 256 * h + LANES:256 * (h + 1)], cos, sin)]
    o_refs[0][...] = (jnp.concatenate(pieces, axis=1) * scale).astype(o_refs[0].dtype)


def _ep_expand_kv(accs, e_refs, o_refs):
    k_nope = accs[0].astype(BF16)
    pe = e_refs[0][...]
    pieces = []
    for h in range(k_nope.shape[1] // LANES):
        pieces += [k_nope[:, LANES * h:LANES * (h + 1)], pe]
    o_refs[0][...] = jnp.concatenate(pieces, axis=1)
    o_refs[1][...] = accs[1].astype(BF16)


def _attn_prompt_kernel(q_ref, k_ref, v_ref, o_ref, *, tq):
    T = q_ref.shape[0]
    causal = _iota2((tq, tq), 1) <= _iota2((tq, tq), 0)
    for h in range(q_ref.shape[1] // 256):
        qk = slice(256 * h, 256 * (h + 1))
        vo = slice(LANES * h, LANES * (h + 1))
        for qi in reversed(range(T // tq)):
            n = (qi + 1) * tq
            q = q_ref[qi * tq:(qi + 1) * tq, qk]
            s = _dg(q, k_ref[:n, qk], _NT)
            diag = jnp.where(causal, s[:, n - tq:], NEG_INF)
            s = diag if qi == 0 else jnp.concatenate([s[:, :n - tq], diag], axis=1)
            m = jnp.max(s, axis=-1, keepdims=True)
            p = jnp.exp(s - m)
            l = jnp.sum(p, axis=-1, keepdims=True)
            o = jnp.dot(p.astype(BF16), v_ref[:n, vo], preferred_element_type=F32)
            o_ref[qi * tq:(qi + 1) * tq, vo] = (o / l).astype(o_ref.dtype)


def attn_prompt_call(q, k, v, *, B, T, H):
    tq = _pick(T, (512, 256, 128))
    assert T % tq == 0 and tq % LANES == 0
    hb = 2 if H % 2 == 0 else 1
    return pl.pallas_call(
        functools.partial(_attn_prompt_kernel, tq=tq),
        grid=(B, H // hb),
        in_specs=[pl.BlockSpec((T, hb * 256), lambda b, h: (b, h)), pl.BlockSpec((T, hb * 256), lambda b, h: (b, h)),
                  pl.BlockSpec((T, hb * LANES), lambda b, h: (b, h))],
        out_specs=pl.BlockSpec((T, hb * LANES), lambda b, h: (b, h)),
        out_shape=jax.ShapeDtypeStruct((B * T, H * LANES), BF16),
        compiler_params=_cparams(("parallel", "parallel")), name="attn_prompt",
    )(q, k, v)


def _absorb_q_kernel(q_ref, wk_ref, o_ref):
    q = q_ref[...]
    lat = jnp.dot(q[:, :LANES], wk_ref[0], preferred_element_type=F32)
    o_ref[...] = jnp.concatenate([lat.astype(BF16), q[:, LANES:]], axis=1)


def absorb_q_call(q, wk_t, *, H):
    M = q.shape[0]
    C_KV = wk_t.shape[2]
    return pl.pallas_call(
        _absorb_q_kernel, grid=(H,),
        in_specs=[pl.BlockSpec((M, 256), lambda h: (0, h)), pl.BlockSpec((1, LANES, C_KV), lambda h: (h, 0, 0))],
        out_specs=pl.BlockSpec((M, C_KV + LANES), lambda h: (0, h)),
        out_shape=jax.ShapeDtypeStruct((M, H * (C_KV + LANES)), BF16),
        compiler_params=_cparams(("parallel",)), name="absorb_q",
    )(q, wk_t)


def _paged_attn_kernel(pt_ref, q_ref, cn_ref, kn_ref, ckv_hbm, kpe_hbm, o_ref, ckv_buf, kpe_buf, sem,
                       m_ref, l_ref, acc_ref, *, P, C_KV, H, n_chain):
    b, j = pl.program_id(0), pl.program_id(1)
    nb, nj = pl.num_programs(0), pl.num_programs(1)
    step = b * nj + j
    slot = lax.rem(step, 2)
    is_last = step == nb * nj - 1
    wrap = j + 1 == nj
    b_nxt = jnp.where(is_last, b, jnp.where(wrap, b + 1, b))
    j_nxt = jnp.where(is_last, j, jnp.where(wrap, 0, j + 1))
    per = P // n_chain

    def page_copies(bb, jj, sl, i):
        page = pt_ref[bb, jj * P + i]
        return (pltpu.make_async_copy(ckv_hbm.at[page], ckv_buf.at[sl, i], sem.at[sl, 0]),
                pltpu.make_async_copy(kpe_hbm.at[page], kpe_buf.at[sl, i], sem.at[sl, 1]))

    def start_pages(bb, jj, sl, lo, hi):
        for i in range(lo, hi):
            for cp in page_copies(bb, jj, sl, i):
                cp.start()

    def wait_pages(bb, jj, sl):
        for i in range(P):
            for cp in page_copies(bb, jj, sl, i):
                cp.wait()

    @pl.when(step == 0)
    def _():
        start_pages(b, j, slot, 0, P)

    @pl.when(j == 0)
    def _():
        m_ref[...] = jnp.full(m_ref.shape, NEG_INF, F32)
        l_ref[...] = jnp.zeros(l_ref.shape, F32)
        acc_ref[...] = jnp.zeros(acc_ref.shape, F32)

    q = q_ref[0]
    q_lat, q_pe = q[:, :C_KV], q[:, C_KV:C_KV + ROPE_DIM]

    def update(c, s, vals):
        m_old = m_ref[c]
        m_new = jnp.maximum(m_old, jnp.max(s, axis=-1, keepdims=True))
        corr = jnp.exp(m_old - m_new)
        p = jnp.exp(s - m_new)
        l_ref[c] = l_ref[c] * corr + jnp.sum(p, axis=-1, keepdims=True)
        acc_ref[c] = acc_ref[c] * corr + jnp.dot(p.astype(BF16), vals, preferred_element_type=F32)
        m_ref[c] = m_new

    wait_pages(b, j, slot)
    kcs, scores = [], []
    for c in range(n_chain):
        start_pages(b_nxt, j_nxt, 1 - slot, c * per, (c + 1) * per)
        pages = range(c * per, (c + 1) * per)
        kc = jnp.concatenate([ckv_buf[slot, i].astype(BF16) for i in pages], axis=0)
        kr = jnp.concatenate([kpe_buf[slot, i].astype(BF16) for i in pages], axis=1)
        kcs.append(kc)
        scores.append(_dg(q_lat, kc, _NT) + _dg(q_pe, kr, _NN))
    for c in range(n_chain):
        update(c, scores[c], kcs[c])

    @pl.when(is_last)
    def _():
        wait_pages(b_nxt, j_nxt, 1 - slot)

    @pl.when(j == nj - 1)
    def _():
        cn = cn_ref[0]
        kn = kn_ref[0][:, :ROPE_DIM]
        sn = _dg(q_lat, cn, _NT) + _dg(q_pe, kn, _NT)
        sn = jnp.where(_iota2(sn.shape, 1) <= _iota2(sn.shape, 0) // H, sn, NEG_INF)
        update(0, sn, cn)
        m = m_ref[0]
        for c in range(1, n_chain):
            m = jnp.maximum(m, m_ref[c])
        acc = jnp.zeros(acc_ref.shape[1:], F32)
        l = jnp.zeros(l_ref.shape[1:], F32)
        for c in range(n_chain):
            wgt = jnp.exp(m_ref[c] - m)
            acc = acc + acc_ref[c] * wgt
            l = l + l_ref[c] * wgt
        o_ref[0] = (acc / l).astype(o_ref.dtype)


def paged_attn_call(qcat, ckv_new, kpe_new, cache_ckv, cache_kpe_t, page_table, *, H):
    Bd, R, QW = qcat.shape
    C_KV = QW - LANES
    n_pages = page_table.shape[1]
    P = _pick(n_pages, (32, 16, 8, 4, 2))
    n_chain = _pick(P, (4, 2, 1))
    page = cache_ckv.shape[1]

    in_specs = [pl.BlockSpec((1, R, QW), lambda b, j, pt: (b, 0, 0)),
                pl.BlockSpec((1, 8, C_KV), lambda b, j, pt: (b, 0, 0)),
                pl.BlockSpec((1, 8, LANES), lambda b, j, pt: (b, 0, 0)),
                pl.BlockSpec(memory_space=pl.ANY), pl.BlockSpec(memory_space=pl.ANY)]
    grid_spec = pltpu.PrefetchScalarGridSpec(
        num_scalar_prefetch=1, grid=(Bd, n_pages // P), in_specs=in_specs,
        out_specs=pl.BlockSpec((1, R, C_KV), lambda b, j, pt: (b, 0, 0)),
        scratch_shapes=[pltpu.VMEM((2, P, page, C_KV), F32), pltpu.VMEM((2, P, ROPE_DIM, page), F32),
                        pltpu.SemaphoreType.DMA((2, 2)),
                        pltpu.VMEM((n_chain, R, 1), F32), pltpu.VMEM((n_chain, R, 1), F32),
                        pltpu.VMEM((n_chain, R, C_KV), F32)])
    return pl.pallas_call(
        functools.partial(_paged_attn_kernel, P=P, C_KV=C_KV, H=H, n_chain=n_chain),
        grid_spec=grid_spec, out_shape=jax.ShapeDtypeStruct((Bd, R, C_KV), BF16),
        compiler_params=_cparams(("arbitrary", "arbitrary")), name="paged_attn",
    )(page_table, qcat, ckv_new, kpe_new, cache_ckv, cache_kpe_t)


def _unabsorb_kernel(o_ref, wv_ref, out_ref):
    out_ref[...] = jnp.dot(o_ref[...], wv_ref[0], preferred_element_type=F32).astype(out_ref.dtype)


def unabsorb_call(o_lat, wv, *, H):
    M = o_lat.shape[0]
    C_KV = wv.shape[1]
    return pl.pallas_call(
        _unabsorb_kernel, grid=(H,),
        in_specs=[pl.BlockSpec((M, C_KV), lambda h: (0, h)), pl.BlockSpec((1, C_KV, LANES), lambda h: (h, 0, 0))],
        out_specs=pl.BlockSpec((M, LANES), lambda h: (0, h)),
        out_shape=jax.ShapeDtypeStruct((M, H * LANES), BF16),
        compiler_params=_cparams(("parallel",)), name="unabsorb_o",
    )(o_lat, wv)


def _pad_cols(w, n):
    return w if w.shape[1] == n else jnp.pad(w, ((0, 0), (0, n - w.shape[1])))


def _pad_rows(w, n):
    return w if w.shape[0] == n else jnp.pad(w, ((0, n - w.shape[0]), (0, 0)))


def _round_up(n, m):
    return -(-n // m) * m


def _prep_weights(p):
    w = {"q_w_a": p["q_w_a"].astype(BF16)}
    for n1, n2 in (("rw_w1", "rw_w2"), ("rw_a1", "rw_a2"), ("rw_g1", "rw_g2")):
        R = _round_up(p[n1].shape[2], LANES)
        w[n1] = [_pad_cols(x, R).astype(BF16) for x in p[n1]]
        w[n2] = [_pad_rows(x, R).astype(BF16) for x in p[n2]]
    C_KV = p["kv_g_a"].shape[0]
    kv_w_a = p["kv_w_a"]
    w["kv_w_a"] = jnp.concatenate([kv_w_a, _rot_cols(kv_w_a[:, C_KV:])], axis=1).astype(BF16)
    H = p["kv_w_b"].shape[1]
    nope = p["kv_w_b"].shape[2] - LANES
    assert nope == LANES and H % 2 == 0 and kv_w_a.shape[1] == C_KV + ROPE_DIM
    Bn, QL, _ = p["q_w_b"].shape
    qb = p["q_w_b"].reshape(Bn, QL, H, nope + ROPE_DIM)
    qb = jnp.concatenate([qb, _rot_cols(qb[..., nope:])], axis=-1)
    w["q_w_b"] = [qb[l].reshape(QL, H * 256).astype(BF16) for l in range(Bn)]
    wk = p["kv_w_b"][..., :nope]
    wv = p["kv_w_b"][..., nope:]
    w["wk_cols"] = wk.reshape(C_KV, H * LANES).astype(BF16)
    w["wv_cols"] = wv.reshape(C_KV, H * LANES).astype(BF16)
    w["wk_t"] = jnp.transpose(wk, (1, 2, 0)).astype(BF16)
    w["wv_h"] = jnp.transpose(wv, (1, 0, 2)).astype(BF16)
    return w


class _BigWeights:
    def __init__(self, p):
        self.p = p
        self.copies = {}
        self.emitting = True

    def matmul(self, a, names, l, *, tn, tn_emit=None, **kw):
        if self.emitting:
            res = matmul_call(a, [(self.p[n], l) for n in names], tn=tn_emit or tn, emit_bf16=True, **kw)
            for n, c in zip(names, res[-len(names):]):
                self.copies[(n, l)] = c
            return res[:-len(names)]
        return matmul_call(a, [self.copies[(n, l)] for n in names], tn=tn, **kw)


def _ffn(x, mix, l, p, big, tm):
    x, h = norm_call(x, mix, p["g_mix_post"][l], [p["g_ffn_pre"][l]], [BF16])
    M = x.shape[0]
    F = p["ffn_w_gate"].shape[2]
    act = big.matmul(h, ["ffn_w_gate", "ffn_w_up"], l, tm=tm, tn=256, epilogue=_ep_swiglu, name="ffn_gate_up",
                     outs=lambda tm_, tn_: [((M, F), BF16, (tm_, tn_), lambda i, j: (i, j))])[0]
    y = big.matmul(act, ["ffn_w_down"], l, tm=min(tm, 512), tn=512, tn_emit=LANES, name="ffn_down")[0]
    return x, y


def _rwkv_layer(x, shift_prev, s0, p, w, big, *, B, T, tm, chunked):
    l = 0
    M, D = x.shape
    if chunked:
        m_r, m_w, m_k, m_v, m_a, m_g = norm_mix_call(x, p["g_mix_pre"][l], p["rw_mu"][l], seq_len=T)
        (shift_out,) = norm_call(x.reshape(B, T, D)[:, -1], pre_gains=[p["g_mix_pre"][l]], pre_dtypes=[F32])
    else:
        (h,) = norm_call(x, pre_gains=[p["g_mix_pre"][l]], pre_dtypes=[F32])
        h3 = h.reshape(B, T, D)
        hp = jnp.concatenate([shift_prev[:, None, :], h3[:, :-1]], axis=1).reshape(M, D)
        shift_out = h3[:, -1]
        m_r, m_w, m_k, m_v, m_a, m_g = mix_call(h, hp, p["rw_mu"][l])
    r = big.matmul(m_r, ["rw_w_r"], l, tm=tm, tn=512, name="rwkv_r")[0]
    k = big.matmul(m_k, ["rw_w_k"], l, tm=tm, tn=512, name="rwkv_k")[0]
    v = big.matmul(m_v, ["rw_w_v"], l, tm=tm, tn=512, name="rwkv_v")[0]
    lw = lora_call(m_w, w["rw_w1"][l], w["rw_w2"][l], p["rw_w0"][l], "decay")
    a = lora_call(m_a, w["rw_a1"][l], w["rw_a2"][l], p["rw_a0"][l], "iclr")
    g = lora_call(m_g, w["rw_g1"][l], w["rw_g2"][l], jnp.zeros((D,), F32), "gate")
    pars = (p["rw_k_k"][l], p["rw_k_a"][l], p["rw_r_k"][l].reshape(D), p["rw_ln_w"][l], p["rw_ln_b"][l])
    if chunked:
        yg, s_new = wkv_chunk_call(r, k, v, lw, a, g, *pars, T=T)
    else:
        tmaj = lambda z: jnp.transpose(z.reshape(B, T, D), (1, 2, 0))
        yg_t, s_t = wkv_step_call(*[tmaj(z) for z in (r, k, v, lw, a, g)], *pars,
                                  jnp.transpose(s0, (1, 2, 3, 0)), T=T)
        yg = jnp.transpose(yg_t, (2, 0, 1)).reshape(M, D)
        s_new = jnp.transpose(s_t, (3, 0, 1, 2))
    mix = big.matmul(yg, ["rw_w_o"], l, tm=tm, tn=512, name="rwkv_o")[0]
    return mix, s_new, shift_out


def _mla_common(x_h1, hkv, p, w, j, *, pos0, B, T, tm, scale):
    M = x_h1.shape[0]
    cos, sin = _rope_tables(pos0, T)
    cos = jnp.tile(cos, (B, 1))
    sin = jnp.tile(sin, (B, 1))
    ckv, kpe, ckv_b, kpe_b = kv_call(hkv, w["kv_w_a"], p["kv_g_a"], cos, sin)
    QL = w["q_w_a"].shape[2]
    tma = min(tm, 512, M)
    qa = matmul_call(x_h1, [(w["q_w_a"], j)], tm=tma, tn=QL, epilogue=_ep_rmsnorm, name="q_down",
                     extras=[(p["q_g_a"][j].reshape(1, QL), (1, QL), lambda i, jj: (0, 0))],
                     outs=[((M, QL), BF16, (tma, QL), lambda i, jj: (i, 0))])[0]
    HN = w["q_w_b"][j].shape[1]
    tmq = min(tm, M)
    tnq = _pick(HN, (1024, 512, 256))
    q = matmul_call(qa, [w["q_w_b"][j]], tm=tmq, tn=tnq, epilogue=functools.partial(_ep_rope_q, scale=scale),
                    name="q_up_rope",
                    extras=[(cos, (tmq, LANES), lambda i, jj: (i, 0)), (sin, (tmq, LANES), lambda i, jj: (i, 0))],
                    outs=[((M, HN), BF16, (tmq, tnq), lambda i, jj: (i, jj))])[0]
    return ckv, kpe, ckv_b, kpe_b, q


def kernel(x_prompt, x_sample, state_wkv, state_shift, cache_ckv, cache_kpe, page_table, g_mix_pre, g_mix_post,
           g_ffn_pre, g_ffn_post, rw_mu, rw_w_r, rw_w_k, rw_w_v, rw_w_o, rw_w0, rw_w1, rw_w2, rw_a0, rw_a1, rw_a2,
           rw_g1, rw_g2, rw_k_k, rw_k_a, rw_r_k, rw_ln_w, rw_ln_b, kv_g_in, kv_w_a, kv_g_a, kv_w_b, q_w_a, q_g_a,
           q_w_b, mla_w_o, ffn_w_gate, ffn_w_up, ffn_w_down):
    p = dict(g_mix_pre=g_mix_pre, g_mix_post=g_mix_post, g_ffn_pre=g_ffn_pre, g_ffn_post=g_ffn_post,
             rw_mu=rw_mu, rw_w_r=rw_w_r, rw_w_k=rw_w_k, rw_w_v=rw_w_v, rw_w_o=rw_w_o,
             rw_w0=rw_w0, rw_w1=rw_w1, rw_w2=rw_w2, rw_a0=rw_a0, rw_a1=rw_a1, rw_a2=rw_a2,
             rw_g1=rw_g1, rw_g2=rw_g2, rw_k_k=rw_k_k, rw_k_a=rw_k_a, rw_r_k=rw_r_k,
             rw_ln_w=rw_ln_w, rw_ln_b=rw_ln_b, kv_g_in=kv_g_in, kv_w_a=kv_w_a, kv_g_a=kv_g_a, kv_w_b=kv_w_b,
             q_w_a=q_w_a, q_g_a=q_g_a, q_w_b=q_w_b, mla_w_o=mla_w_o,
             ffn_w_gate=ffn_w_gate, ffn_w_up=ffn_w_up, ffn_w_down=ffn_w_down)
    assert g_mix_pre.shape[0] == 2 and rw_mu.shape[0] == 1 and q_w_a.shape[0] == 1, "one RWKV-7 + one MLA layer"
    assert cache_ckv.shape[1] == PAGE_SIZE
    w = _prep_weights(p)
    D = x_prompt.shape[-1]
    H = kv_w_b.shape[1]
    C_KV = kv_g_a.shape[0]
    scale = float((LANES + ROPE_DIM) ** -0.5)
    outs = {}
    big = _BigWeights(p)
    for grp in ("sample", "prompt"):
        big.emitting = grp == "sample"
        if grp == "prompt":
            x3 = x_prompt
            B, T, _ = x3.shape
            s0 = None
            shift0 = jnp.zeros((B, D), F32)
            pos0 = 0
        else:
            x3 = x_sample
            B, T, _ = x3.shape
            s0 = state_wkv[0]
            shift0 = state_shift[0]
            pos0 = page_table.shape[1] * PAGE_SIZE
        M = B * T
        tm = _pick(M, (1024, 512, 256, 128, 64, 32, 16, 8))
        x = x3.reshape(M, D)
        mix, s_new, shift_out = _rwkv_layer(x, shift0, s0, p, w, big, B=B, T=T, tm=tm, chunked=(grp == "prompt"))
        x, y = _ffn(x, mix, 0, p, big, tm)
        x, h1, hkv = norm_call(x, y, g_ffn_post[0], [g_mix_pre[1], kv_g_in], [BF16, BF16])
        ckv, kpe, ckv_b, kpe_b, q = _mla_common(h1, hkv, p, w, 0, pos0=pos0, B=B, T=T, tm=tm, scale=scale)
        if grp == "prompt":
            tme = min(tm, M)
            tne = _pick(H * LANES, (1024, 512, 256))
            k_all, v_all = matmul_call(
                ckv_b, [w["wk_cols"], w["wv_cols"]], tm=tme, tn=tne, epilogue=_ep_expand_kv, name="kv_expand",
                extras=[(kpe_b, (tme, LANES), lambda i, jj: (i, 0))],
                outs=[((M, H * 256), BF16, (tme, 2 * tne), lambda i, jj: (i, jj)),
                      ((M, H * LANES), BF16, (tme, tne), lambda i, jj: (i, jj))])
            o = attn_prompt_call(q, k_all, v_all, B=B, T=T, H=H)
        else:
            qcat = absorb_q_call(q, w["wk_t"], H=H).reshape(B, T * H, C_KV + LANES)
            pad8 = lambda z: jnp.pad(z.reshape(B, T, z.shape[-1]), ((0, 0), (0, 8 - T), (0, 0)))
            o_lat = paged_attn_call(qcat, pad8(ckv_b), pad8(kpe_b), cache_ckv, jnp.transpose(cache_kpe, (0, 2, 1)),
                                    page_table, H=H)
            o = unabsorb_call(o_lat.reshape(M, H * C_KV), w["wv_h"], H=H)
        mix = big.matmul(o, ["mla_w_o"], 0, tm=min(tm, 512), tn=512, tn_emit=256, name="mla_out")[0]
        x, y = _ffn(x, mix, 1, p, big, tm)
        (x,) = norm_call(x, y, g_ffn_post[1])
        outs[grp] = (x.reshape(B, T, D), s_new[None], shift_out[None], ckv.reshape(B, T, C_KV),
                     kpe.reshape(B, T, ROPE_DIM))
    yp, wp, sp, cp, kp = outs["prompt"]
    ys, ws, ss, cs, ks = outs["sample"]
    return (yp, ys, wp, sp, cp, kp, ws, ss, cs, ks)
```

```python
import functools
import math

import jax
import jax.numpy as jnp
from jax import lax
from jax.experimental import pallas as pl
from jax.experimental.pallas import tpu as pltpu

F32 = jnp.float32
BF16 = jnp.bfloat16

V7X_VMEM_BYTES = 64 * 1024 * 1024
LANES = 128
VMEM_LIMIT = V7X_VMEM_BYTES - 8 * 1024 * 1024

HEAD_A = 64
WKV_CHUNK = 64
WKV_GROUP = 256
GN_EPS = 64e-5
RMS_EPS = 1e-6
ROPE_THETA = 10000.0
ROPE_DIM = 64
PAGE_SIZE = 128
NEG_INF = float("-inf")


def _cparams(sem):
    return pltpu.CompilerParams(dimension_semantics=sem, vmem_limit_bytes=VMEM_LIMIT)


def _pick(n, cands):
    for c in cands:
        if n % c == 0:
            return c
    return n


def _rms(x, g):
    return x * lax.rsqrt(jnp.mean(x * x, axis=-1, keepdims=True) + RMS_EPS) * g


_NN = (((1,), (0,)), ((), ()))
_NT = (((1,), (1,)), ((), ()))
_TN = (((0,), (0,)), ((), ()))


def _dg(a, b, dims):
    return lax.dot_general(a, b, dims, preferred_element_type=F32)


def _iota2(shape, dim):
    return lax.broadcasted_iota(jnp.int32, shape, dim)


def _norm_kernel(*refs, has_add, n_pre, pre_dtypes):
    it = iter(refs)
    x_ref = next(it)
    if has_add:
        mix_ref = next(it)
        gpost_ref = next(it)
    gpre_refs = [next(it) for _ in range(n_pre)]
    x = x_ref[...]
    if has_add:
        x = x + _rms(mix_ref[...], gpost_ref[...])
        xo_ref = next(it)
        xo_ref[...] = x
    if n_pre:
        xn = x * lax.rsqrt(jnp.mean(x * x, axis=-1, keepdims=True) + RMS_EPS)
        for g_ref, dt in zip(gpre_refs, pre_dtypes):
            o_ref = next(it)
            o_ref[...] = (xn * g_ref[...]).astype(dt)


def norm_call(x, mix=None, g_post=None, pre_gains=(), pre_dtypes=()):
    M, D = x.shape
    tr = _pick(M, (256, 128, 64, 32, 16, 8))
    row = pl.BlockSpec((tr, D), lambda i: (i, 0))
    vec = pl.BlockSpec((1, D), lambda i: (0, 0))
    has_add = mix is not None
    ins, in_specs = [x], [row]
    if has_add:
        ins += [mix, g_post.reshape(1, D)]
        in_specs += [row, vec]
    for g in pre_gains:
        ins.append(g.reshape(1, D))
        in_specs.append(vec)
    out_shape, out_specs = [], []
    if has_add:
        out_shape.append(jax.ShapeDtypeStruct((M, D), F32))
        out_specs.append(row)
    for dt in pre_dtypes:
        out_shape.append(jax.ShapeDtypeStruct((M, D), dt))
        out_specs.append(row)
    return pl.pallas_call(
        functools.partial(_norm_kernel, has_add=has_add, n_pre=len(pre_gains), pre_dtypes=tuple(pre_dtypes)),
        grid=(M // tr,), in_specs=in_specs, out_specs=out_specs, out_shape=out_shape,
        compiler_params=_cparams(("parallel",)), name="rmsnorm",
    )(*ins)


def _mix_kernel(h_ref, hp_ref, mu_ref, *o_refs):
    h = h_ref[...]
    dx = hp_ref[...] - h
    for j, o_ref in enumerate(o_refs):
        o_ref[...] = (h + dx * mu_ref[j:j + 1, :]).astype(BF16)


def mix_call(h, hp, mu):
    M, D = h.shape
    n = mu.shape[0]
    tr = _pick(M, (256, 128, 64, 32, 16, 8))
    row = pl.BlockSpec((tr, D), lambda i: (i, 0))
    return pl.pallas_call(
        _mix_kernel, grid=(M // tr,),
        in_specs=[row, row, pl.BlockSpec((n, D), lambda i: (0, 0))],
        out_specs=[row] * n, out_shape=[jax.ShapeDtypeStruct((M, D), BF16)] * n,
        compiler_params=_cparams(("parallel",)), name="token_shift_mix",
    )(h, hp, mu)


def _norm_mix_kernel(x_ref, xp_ref, g_ref, mu_ref, *o_refs, seq_len):
    tr = x_ref.shape[0]
    g = g_ref[...]
    h = _rms(x_ref[...], g)
    prev_last = _rms(xp_ref[...], g)[7:8, :]
    at_seq_start = lax.rem(pl.program_id(0) * tr, seq_len) == 0
    prev_last = jnp.where(at_seq_start, 0.0, prev_last)
    hp = jnp.where(_iota2(h.shape, 0) == 0, prev_last, pltpu.roll(h, 1, 0))
    dx = hp - h
    for j, o_ref in enumerate(o_refs):
        o_ref[...] = (h + dx * mu_ref[j:j + 1, :]).astype(BF16)


def norm_mix_call(x, g, mu, *, seq_len):
    M, D = x.shape
    n = mu.shape[0]
    tr = _pick(seq_len, (256, 128, 64, 32, 16, 8))
    assert seq_len % tr == 0 and M % seq_len == 0 and tr % 8 == 0
    row = pl.BlockSpec((tr, D), lambda i: (i, 0))
    return pl.pallas_call(
        functools.partial(_norm_mix_kernel, seq_len=seq_len), grid=(M // tr,),
        in_specs=[row, pl.BlockSpec((8, D), lambda i: (jnp.maximum(i * (tr // 8) - 1, 0), 0)),
                  pl.BlockSpec((1, D), lambda i: (0, 0)), pl.BlockSpec((n, D), lambda i: (0, 0))],
        out_specs=[row] * n, out_shape=[jax.ShapeDtypeStruct((M, D), BF16)] * n,
        compiler_params=_cparams(("parallel",)), name="norm_shift_mix",
    )(x, x, g.reshape(1, D), mu)


def _matmul_kernel(a_ref, *refs, nb, ne, epilogue, emit_bf16):
    b_refs, e_refs, o_refs = refs[:nb], refs[nb:nb + ne], refs[nb + ne:]
    a = a_ref[...]
    bvals = [b[...].astype(BF16) for b in b_refs]
    accs = [jnp.dot(a, b, preferred_element_type=F32) for b in bvals]
    if emit_bf16:
        for b, o_ref in zip(bvals, o_refs[-nb:]):
            o_ref[...] = b
        o_refs = o_refs[:-nb]
    epilogue(accs, e_refs, o_refs)


def _ep_store(accs, e_refs, o_refs):
    o_refs[0][...] = accs[0].astype(o_refs[0].dtype)


def _ep_swiglu(accs, e_refs, o_refs):
    g, u = accs
    o_refs[0][...] = (g * jax.nn.sigmoid(g) * u).astype(o_refs[0].dtype)


def matmul_call(a, bs, *, tm, tn, name, epilogue=_ep_store, extras=(), outs=None, emit_bf16=False):
    M, K = a.shape
    bs = [b if isinstance(b, tuple) else (b[None], 0) for b in bs]
    N = bs[0][0].shape[2]
    tm, tn = min(tm, M), min(tn, N)
    assert M % tm == 0 and N % tn == 0, (M, N, tm, tn)
    if outs is None:
        outs = [((M, N), F32, (tm, tn), lambda i, j: (i, j))]
    elif callable(outs):
        outs = outs(tm, tn)
    if emit_bf16:
        assert M == tm, "every weight tile must be visited exactly once"
        outs = list(outs) + [((K, N), BF16, (K, tn), lambda i, j: (0, j)) for _ in bs]
    in_specs = [pl.BlockSpec((tm, K), lambda i, j: (i, 0))]
    in_specs += [pl.BlockSpec((None, K, tn), lambda i, j, l=l: (l, 0, j)) for _, l in bs]
    bs = [b for b, _ in bs]
    in_specs += [pl.BlockSpec(bshape, imap) for _, bshape, imap in extras]
    return pl.pallas_call(
        functools.partial(_matmul_kernel, nb=len(bs), ne=len(extras), epilogue=epilogue, emit_bf16=emit_bf16),
        grid=(M // tm, N // tn), in_specs=in_specs,
        out_specs=[pl.BlockSpec(bshape, imap) for _, _, bshape, imap in outs],
        out_shape=[jax.ShapeDtypeStruct(shape, dt) for shape, dt, _, _ in outs],
        compiler_params=_cparams(("parallel", "arbitrary")), name=name,
    )(a, *bs, *[e[0] for e in extras])


def _lora_kernel(a_ref, w1_ref, w2_ref, bias_ref, o_ref, *, kind):
    z = jnp.dot(a_ref[...], w1_ref[...], preferred_element_type=F32)
    if kind == "decay":
        z = jnp.tanh(z)
    elif kind == "gate":
        z = jax.nn.sigmoid(z)
    y = jnp.dot(z.astype(BF16), w2_ref[...], preferred_element_type=F32)
    if kind == "decay":
        y = jax.nn.sigmoid(bias_ref[...] + y) * (-math.exp(-0.5))
    elif kind == "iclr":
        y = jax.nn.sigmoid(bias_ref[...] + y)
    o_ref[...] = y


def lora_call(a, w1, w2, bias, kind):
    M, D = a.shape
    R = w1.shape[1]
    tm = _pick(M, (512, 256, 128, 64, 32, 16, 8))
    return pl.pallas_call(
        functools.partial(_lora_kernel, kind=kind), grid=(M // tm,),
        in_specs=[pl.BlockSpec((tm, D), lambda i: (i, 0)), pl.BlockSpec((D, R), lambda i: (0, 0)),
                  pl.BlockSpec((R, D), lambda i: (0, 0)), pl.BlockSpec((1, D), lambda i: (0, 0))],
        out_specs=pl.BlockSpec((tm, D), lambda i: (i, 0)),
        out_shape=jax.ShapeDtypeStruct((M, D), F32),
        compiler_params=_cparams(("parallel",)), name="lora_" + kind,
    )(a, w1, w2, bias.reshape(1, D))


def _split2(x):
    hi = x.astype(BF16)
    lo = (x - hi.astype(F32)).astype(BF16)
    return hi, lo


def _split3(x):
    h1 = x.astype(BF16)
    r1 = x - h1.astype(F32)
    h2 = r1.astype(BF16)
    h3 = (r1 - h2.astype(F32)).astype(BF16)
    return h1, h2, h3


def _mm_exact3(a, b_bf16, dims):
    h1, h2, h3 = _split3(a)
    return _dg(h1, b_bf16, dims) + (_dg(h2, b_bf16, dims) + _dg(h3, b_bf16, dims))


def _segsum(x, ones_bd):
    return _dg(x.astype(BF16), ones_bd, _NN)


def _wkv_groups(r, k, v, lw, a, g, kkp, kap, rkp, lnw, lnb, s_bd, ones_bd, tri):
    C = WKV_CHUNK
    G = WKV_GROUP // HEAD_A

    def each(f, *ls):
        return [f(*xs) for xs in zip(*ls)]

    def bd(x):
        return jnp.concatenate([x.astype(BF16)] * G, axis=0) * ones_bd

    def mm(x, y_bd):
        return _dg(x.astype(BF16), y_bd, _NN)

    def cat(x, y):
        return jnp.concatenate([x, y], axis=0)

    kk = each(lambda k_, p_: k_ * p_, k, kkp)
    ssq = each(lambda x: _segsum(x * x, ones_bd), kk)
    kk = each(lambda x, s: x * lax.rsqrt(jnp.maximum(s, 1e-24)), kk, ssq)
    kmod = each(lambda k_, a_, p_: k_ * (1.0 + (a_ - 1.0) * p_), k, a, kap)
    b = each(lambda x, a_: x * a_, kk, a)

    def cumsum(x):
        hi, lo = _split2(x)
        return _dg(tri, hi, _NN) + _dg(tri, lo, _NN)

    L = each(cumsum, lw)
    Ltot = each(lambda x: x[C - 1:C, :], L)
    enL = each(lambda x: jnp.exp(-x), L)
    eR = each(lambda t_, x: jnp.exp(t_ - x), Ltot, L)
    at = each(lambda x, l_, w_: -x * jnp.exp(l_ - w_), kk, L, lw)
    bt = each(lambda x, e: x * e, b, enL)
    kt = each(lambda x, e: x * e, kmod, enL)
    rt = each(lambda x, l_: x * jnp.exp(l_), r, L)
    bh = each(lambda x, e: x * e, b, eR)
    kh = each(lambda x, e: x * e, kmod, eR)

    t_i = _iota2((C, WKV_GROUP), 0)
    s_i = _iota2((C, WKV_GROUP), 1) % C
    lhs2 = each(lambda x, y: cat(x, y).astype(BF16), at, rt)
    p_b = each(lambda x, y: _dg(x, bd(y), _NT), lhs2, bt)
    p_k = each(lambda x, y: _dg(x, bd(y), _NT), lhs2, kt)
    a_ab = each(lambda x: jnp.where(s_i < t_i, x[:C], 0.0), p_b)
    a_rb = each(lambda x: jnp.where(s_i <= t_i, x[C:], 0.0), p_b)
    a_ak = each(lambda x: jnp.where(s_i < t_i, x[:C], 0.0), p_k)
    a_rk = each(lambda x: jnp.where(s_i <= t_i, x[C:], 0.0), p_k)

    eye = jnp.where(s_i == t_i, 1.0, 0.0)
    tinv = each(lambda x: eye + x, a_ab)
    nk = each(lambda x: mm(x, bd(x)), a_ab)
    for _ in range(4):
        both = each(lambda n_, t_: mm(cat(n_, t_), bd(n_)), nk, tinv)
        tinv = each(lambda t_, x: t_ + x[C:], tinv, both)
        nk = each(lambda x: x[:C], both)
    tinv = each(lambda t_, n_: t_ + mm(t_, bd(n_)), tinv, nk)

    sprod = each(lambda x, s: _dg(x, s.astype(BF16), _NT), lhs2, s_bd)
    akrk = each(lambda x, y, v_: mm(cat(x, y), bd(v_)), a_ak, a_rk, v)
    u = each(lambda t_, s, x: mm(t_, bd(s[:C] + x[:C])), tinv, sprod, akrk)
    o = each(lambda s, x, m_, u_: s[C:] + x[C:] + mm(m_, bd(u_)), sprod, akrk, a_rb, u)

    upd = each(lambda u_, v_, b_, k_: _dg(cat(u_, v_).astype(BF16), cat(b_, k_).astype(BF16), _TN),
               u, v, bh, kh)
    mask_f = ones_bd.astype(F32)
    s_new = each(lambda s, t_, x: s * jnp.exp(t_) + x * mask_f, s_bd, Ltot, upd)

    inv_n = 1.0 / HEAD_A
    sums = each(lambda o_, r_, k_, p_: _segsum(cat(o_, r_ * k_ * p_), ones_bd), o, r, kmod, rkp)
    d = each(lambda o_, s: o_ - s[:C] * inv_n, o, sums)
    var = each(lambda x: _segsum(x * x, ones_bd) * inv_n, d)
    y = each(lambda d_, var_, w_, b_, s, v_, g_: (d_ * lax.rsqrt(var_ + GN_EPS) * w_ + b_ + s[C:] * v_) * g_,
             d, var, lnw, lnb, sums, v, g)
    return y, s_new


def _wkv_chunk_kernel(r_ref, k_ref, v_ref, lw_ref, a_ref, g_ref, kk_ref, ka_ref, rk_ref, lnw_ref, lnb_ref,
                      y_ref, st_ref, sbd_ref):
    C = WKV_CHUNK
    ng = r_ref.shape[1] // WKV_GROUP
    c_idx = pl.program_id(2)
    W = WKV_GROUP
    same_head = (_iota2((W, W), 0) // HEAD_A) == (_iota2((W, W), 1) // HEAD_A)
    ones_bd = jnp.where(same_head, 1.0, 0.0).astype(BF16)
    tri = jnp.where(_iota2((C, C), 1) <= _iota2((C, C), 0), 1.0, 0.0).astype(BF16)

    @pl.when(c_idx == 0)
    def _():
        sbd_ref[...] = jnp.zeros(sbd_ref.shape, F32)

    sls = [slice(W * q, W * (q + 1)) for q in range(ng)]
    split = lambda ref: [ref[:, sl] for sl in sls]
    ys, s_news = _wkv_groups(*[split(ref) for ref in (r_ref, k_ref, v_ref, lw_ref, a_ref, g_ref, kk_ref, ka_ref,
                                                      rk_ref, lnw_ref, lnb_ref)],
                             [sbd_ref[q] for q in range(ng)], ones_bd, tri)
    for q in range(ng):
        y_ref[:, sls[q]] = ys[q].astype(y_ref.dtype)
        sbd_ref[q] = s_news[q]

    @pl.when(c_idx == pl.num_programs(2) - 1)
    def _():
        fold = jnp.where((_iota2((HEAD_A, W), 1) % HEAD_A) == _iota2((HEAD_A, W), 0), 1.0, 0.0).astype(BF16)
        for q in range(ng):
            rows = _mm_exact3(sbd_ref[q], fold, _NT)
            st_ref[0, 4 * q:4 * q + 4] = rows.reshape(4, HEAD_A, HEAD_A)


def wkv_chunk_call(r, k, v, lw, a, g, k_k, k_a, r_k, ln_w, ln_b, *, T):
    M, D = r.shape
    B = M // T
    H = D // HEAD_A
    C = WKV_CHUNK
    assert C == HEAD_A and T % C == 0
    Lw = _pick(D, (8 * WKV_GROUP, 4 * WKV_GROUP, 2 * WKV_GROUP, WKV_GROUP))
    assert D % Lw == 0 and Lw % WKV_GROUP == 0
    hb = Lw // HEAD_A
    nc = T // C
    row = pl.BlockSpec((C, Lw), lambda bi, hg, c: (bi * nc + c, hg))
    vec = pl.BlockSpec((1, Lw), lambda bi, hg, c: (0, hg))
    st = pl.BlockSpec((1, hb, HEAD_A, HEAD_A), lambda bi, hg, c: (bi, hg, 0, 0))
    vecs = [x.reshape(1, D) for x in (k_k, k_a, r_k, ln_w, ln_b)]
    return pl.pallas_call(
        _wkv_chunk_kernel,
        grid=(B, D // Lw, nc),
        in_specs=[row] * 6 + [vec] * 5,
        out_specs=[row, st],
        out_shape=[jax.ShapeDtypeStruct((M, D), BF16), jax.ShapeDtypeStruct((B, H, HEAD_A, HEAD_A), F32)],
        scratch_shapes=[pltpu.VMEM((Lw // WKV_GROUP, WKV_GROUP, WKV_GROUP), F32)],
        compiler_params=_cparams(("parallel", "parallel", "arbitrary")), name="wkv_chunk",
    )(r, k, v, lw, a, g, *vecs)


def _wkv_step_kernel(r_ref, k_ref, v_ref, lw_ref, a_ref, g_ref, kk_ref, ka_ref, rk_ref, lnw_ref, lnb_ref,
                     s_ref, y_ref, so_ref, op_ref, sa_ref, o_ref, *, T, hb):
    N = HEAD_A
    for h in range(hb):
        sl = slice(h * N, (h + 1) * N)
        for t in range(T):
            k = k_ref[t, sl, :]
            a = a_ref[t, sl, :]
            kk = k * kk_ref[sl, :]
            kk = kk * lax.rsqrt(jnp.maximum(jnp.sum(kk * kk, axis=0, keepdims=True), 1e-24))
            op_ref[t, 0] = jnp.exp(lw_ref[t, sl, :])
            op_ref[t, 1] = kk * a
            op_ref[t, 2] = k * (1.0 + (a - 1.0) * ka_ref[sl, :])
            op_ref[t, 3] = r_ref[t, sl, :]
            op_ref[t, 4] = -kk

        def first(vi, carry, h=h):
            sa_ref[0, pl.ds(vi, 1), :] = jnp.sum(s_ref[h, vi] * op_ref[0, 4], axis=0, keepdims=True)
            return carry

        lax.fori_loop(0, N, first, 0, unroll=8)
        for t in range(T):
            src = s_ref if t == 0 else so_ref
            cur, nxt = t % 2, (t + 1) % 2

            def body(vi, carry, h=h, t=t, src=src, cur=cur, nxt=nxt):
                s_row = src[h, vi]
                sa_row = sa_ref[cur, pl.ds(vi, 1), :]
                v_row = v_ref[t, pl.ds(h * N + vi, 1), :]
                s_new = s_row * op_ref[t, 0] + sa_row * op_ref[t, 1] + v_row * op_ref[t, 2]
                so_ref[h, vi] = s_new
                o_ref[pl.ds(vi, 1), :] = jnp.sum(s_new * op_ref[t, 3], axis=0, keepdims=True)
                if t + 1 < T:
                    sa_ref[nxt, pl.ds(vi, 1), :] = jnp.sum(s_new * op_ref[t + 1, 4], axis=0, keepdims=True)
                return carry

            lax.fori_loop(0, N, body, 0, unroll=2)
            o = o_ref[...]
            d = o - jnp.mean(o, axis=0, keepdims=True)
            var = jnp.mean(d * d, axis=0, keepdims=True)
            o_n = d * lax.rsqrt(var + GN_EPS) * lnw_ref[sl, :] + lnb_ref[sl, :]
            bonus = jnp.sum(op_ref[t, 3] * op_ref[t, 2] * rk_ref[sl, :], axis=0, keepdims=True) * v_ref[t, sl, :]
            y_ref[t, sl, :] = ((o_n + bonus) * g_ref[t, sl, :]).astype(y_ref.dtype)


def wkv_step_call(r, k, v, lw, a, g, k_k, k_a, r_k, ln_w, ln_b, s0, *, T):
    _, D, Bd = r.shape
    H = D // HEAD_A
    hb = 2 if H % 2 == 0 else 1
    seq = pl.BlockSpec((T, hb * HEAD_A, Bd), lambda i: (0, i, 0))
    par = pl.BlockSpec((hb * HEAD_A, Bd), lambda i: (i, 0))
    st = pl.BlockSpec((hb, HEAD_A, HEAD_A, Bd), lambda i: (i, 0, 0, 0))
    pars = [jnp.broadcast_to(x.reshape(D, 1), (D, Bd)) for x in (k_k, k_a, r_k, ln_w, ln_b)]
    return pl.pallas_call(
        functools.partial(_wkv_step_kernel, T=T, hb=hb), grid=(H // hb,),
        in_specs=[seq] * 6 + [par] * 5 + [st],
        out_specs=[seq, st],
        out_shape=[jax.ShapeDtypeStruct((T, D, Bd), BF16), jax.ShapeDtypeStruct((H, HEAD_A, HEAD_A, Bd), F32)],
        scratch_shapes=[pltpu.VMEM((T, 5, HEAD_A, Bd), F32), pltpu.VMEM((2, HEAD_A, Bd), F32),
                        pltpu.VMEM((HEAD_A, Bd), F32)],
        compiler_params=_cparams(("parallel",)), name="wkv_step",
    )(r, k, v, lw, a, g, *pars, s0)


def _rope_tables(pos0, T):
    half = ROPE_DIM // 2
    inv = ROPE_THETA ** (-jnp.arange(half, dtype=F32) / half)
    ang = (pos0 + jnp.arange(T)).astype(F32)[:, None] * inv[None, :]
    cos, sin = jnp.cos(ang), jnp.sin(ang)
    return jnp.tile(cos, (1, LANES // half)), jnp.tile(sin, (1, LANES // half))


def _rot_cols(w):
    half = ROPE_DIM // 2
    return jnp.concatenate([-w[..., half:], w[..., :half]], axis=-1)


def _rope_tail(tail, cos, sin):
    rot = pltpu.roll(tail, ROPE_DIM, 1)
    return jnp.where(_iota2(tail.shape, 1) < ROPE_DIM, tail * cos + rot * sin, 0.0)


def _kv_kernel(a_ref, w_ref, g_ref, cos_ref, sin_ref, ckv_ref, kpe_ref, ckvb_ref, kpeb_ref, *, C_KV):
    kv = jnp.dot(a_ref[...], w_ref[...], preferred_element_type=F32)
    ckv = _rms(kv[:, :C_KV], g_ref[...])
    kpe = _rope_tail(kv[:, C_KV:C_KV + LANES], cos_ref[...], sin_ref[...])
    ckv_ref[...] = ckv
    kpe_ref[...] = kpe[:, :ROPE_DIM]
    ckvb_ref[...] = ckv.astype(BF16)
    kpeb_ref[...] = kpe.astype(BF16)


def kv_call(hkv, w_ext, g_a, cos, sin):
    M, D = hkv.shape
    C_KV = g_a.shape[0]
    NW = w_ext.shape[1]
    tm = _pick(M, (512, 256, 128, 64, 32, 16, 8))
    rowb = lambda n: pl.BlockSpec((tm, n), lambda i: (i, 0))
    return pl.pallas_call(
        functools.partial(_kv_kernel, C_KV=C_KV), grid=(M // tm,),
        in_specs=[rowb(D), pl.BlockSpec((D, NW), lambda i: (0, 0)), pl.BlockSpec((1, C_KV), lambda i: (0, 0)),
                  rowb(LANES), rowb(LANES)],
        out_specs=[rowb(C_KV), rowb(ROPE_DIM), rowb(C_KV), rowb(LANES)],
        out_shape=[jax.ShapeDtypeStruct((M, C_KV), F32), jax.ShapeDtypeStruct((M, ROPE_DIM), F32),
                   jax.ShapeDtypeStruct((M, C_KV), BF16), jax.ShapeDtypeStruct((M, LANES), BF16)],
        compiler_params=_cparams(("parallel",)), name="latent_kv",
    )(hkv, w_ext, g_a.reshape(1, C_KV), cos, sin)


def _ep_rmsnorm(accs, e_refs, o_refs):
    o_refs[0][...] = _rms(accs[0], e_refs[0][...]).astype(o_refs[0].dtype)


def _ep_rope_q(accs, e_refs, o_refs, *, scale):
    acc = accs[0]
    cos, sin = e_refs[0][...], e_refs[1][...]
    pieces = []
    for h in range(acc.shape[1] // 256):
        pieces += [acc[:, 256 * h:256 * h + LANES], _rope_tail(acc[:, 256 * h + LANES:256 * (h + 1)], cos, sin)]
    o_refs[0][...] = (jnp.concatenate(pieces, axis=1) * scale).astype(o_refs[0].dtype)


def _ep_expand_kv(accs, e_refs, o_refs):
    k_nope = accs[0].astype(BF16)
    pe = e_refs[0][...]
    pieces = []
    for h in range(k_nope.shape[1] // LANES):
        pieces += [k_nope[:, LANES * h:LANES * (h + 1)], pe]
    o_refs[0][...] = jnp.concatenate(pieces, axis=1)
    o_refs[1][...] = accs[1].astype(BF16)


def _attn_prompt_kernel(q_ref, k_ref, v_ref, o_ref, *, tq):
    T = q_ref.shape[0]
    causal = _iota2((tq, tq), 1) <= _iota2((tq, tq), 0)
    for h in range(q_ref.shape[1] // 256):
        qk = slice(256 * h, 256 * (h + 1))
        vo = slice(LANES * h, LANES * (h + 1))
        for qi in reversed(range(T // tq)):
            n = (qi + 1) * tq
            q = q_ref[qi * tq:(qi + 1) * tq, qk]
            s = _dg(q, k_ref[:n, qk], _NT)
            diag = jnp.where(causal, s[:, n - tq:], NEG_INF)
            s = diag if qi == 0 else jnp.concatenate([s[:, :n - tq], diag], axis=1)
            m = jnp.max(s, axis=-1, keepdims=True)
            p = jnp.exp(s - m)
            l = jnp.sum(p, axis=-1, keepdims=True)
            o = jnp.dot(p.astype(BF16), v_ref[:n, vo], preferred_element_type=F32)
            o_ref[qi * tq:(qi + 1) * tq, vo] = (o / l).astype(o_ref.dtype)


def attn_prompt_call(q, k, v, *, B, T, H):
    tq = _pick(T, (512, 256, 128))
    assert T % tq == 0 and tq % LANES == 0
    hb = 2 if H % 2 == 0 else 1
    return pl.pallas_call(
        functools.partial(_attn_prompt_kernel, tq=tq),
        grid=(B, H // hb),
        in_specs=[pl.BlockSpec((T, hb * 256), lambda b, h: (b, h)), pl.BlockSpec((T, hb * 256), lambda b, h: (b, h)),
                  pl.BlockSpec((T, hb * LANES), lambda b, h: (b, h))],
        out_specs=pl.BlockSpec((T, hb * LANES), lambda b, h: (b, h)),
        out_shape=jax.ShapeDtypeStruct((B * T, H * LANES), BF16),
        compiler_params=_cparams(("parallel", "parallel")), name="attn_prompt",
    )(q, k, v)


def _absorb_q_kernel(q_ref, wk_ref, o_ref):
    q = q_ref[...]
    lat = jnp.dot(q[:, :LANES], wk_ref[0], preferred_element_type=F32)
    o_ref[...] = jnp.concatenate([lat.astype(BF16), q[:, LANES:]], axis=1)


def absorb_q_call(q, wk_t, *, H):
    M = q.shape[0]
    C_KV = wk_t.shape[2]
    return pl.pallas_call(
        _absorb_q_kernel, grid=(H,),
        in_specs=[pl.BlockSpec((M, 256), lambda h: (0, h)), pl.BlockSpec((1, LANES, C_KV), lambda h: (h, 0, 0))],
        out_specs=pl.BlockSpec((M, C_KV + LANES), lambda h: (0, h)),
        out_shape=jax.ShapeDtypeStruct((M, H * (C_KV + LANES)), BF16),
        compiler_params=_cparams(("parallel",)), name="absorb_q",
    )(q, wk_t)


def _paged_attn_kernel(pt_ref, q_ref, cn_ref, kn_ref, ckv_hbm, kpe_hbm, o_ref, ckv_buf, kpe_buf, sem,
                       m_ref, l_ref, acc_ref, *, P, C_KV, H, n_chain):
    b, j = pl.program_id(0), pl.program_id(1)
    nb, nj = pl.num_programs(0), pl.num_programs(1)
    step = b * nj + j
    slot = lax.rem(step, 2)
    is_last = step == nb * nj - 1
    wrap = j + 1 == nj
    b_nxt = jnp.where(is_last, b, jnp.where(wrap, b + 1, b))
    j_nxt = jnp.where(is_last, j, jnp.where(wrap, 0, j + 1))
    per = P // n_chain

    def page_copies(bb, jj, sl, i):
        page = pt_ref[bb, jj * P + i]
        return (pltpu.make_async_copy(ckv_hbm.at[page], ckv_buf.at[sl, i], sem.at[sl, 0]),
                pltpu.make_async_copy(kpe_hbm.at[page], kpe_buf.at[sl, i], sem.at[sl, 1]))

    def start_pages(bb, jj, sl, lo, hi):
        for i in range(lo, hi):
            for cp in page_copies(bb, jj, sl, i):
                cp.start()

    def wait_pages(bb, jj, sl):
        for i in range(P):
            for cp in page_copies(bb, jj, sl, i):
                cp.wait()

    @pl.when(step == 0)
    def _():
        start_pages(b, j, slot, 0, P)

    @pl.when(j == 0)
    def _():
        m_ref[...] = jnp.full(m_ref.shape, NEG_INF, F32)
        l_ref[...] = jnp.zeros(l_ref.shape, F32)
        acc_ref[...] = jnp.zeros(acc_ref.shape, F32)

    q = q_ref[0]
    q_lat, q_pe = q[:, :C_KV], q[:, C_KV:C_KV + ROPE_DIM]

    def update(c, s, vals):
        m_old = m_ref[c]
        m_new = jnp.maximum(m_old, jnp.max(s, axis=-1, keepdims=True))
        corr = jnp.exp(m_old - m_new)
        p = jnp.exp(s - m_new)
        l_ref[c] = l_ref[c] * corr + jnp.sum(p, axis=-1, keepdims=True)
        acc_ref[c] = acc_ref[c] * corr + jnp.dot(p.astype(BF16), vals, preferred_element_type=F32)
        m_ref[c] = m_new

    wait_pages(b, j, slot)
    kcs, scores = [], []
    for c in range(n_chain):
        start_pages(b_nxt, j_nxt, 1 - slot, c * per, (c + 1) * per)
        pages = range(c * per, (c + 1) * per)
        kc = jnp.concatenate([ckv_buf[slot, i].astype(BF16) for i in pages], axis=0)
        kr = jnp.concatenate([kpe_buf[slot, i].astype(BF16) for i in pages], axis=1)
        kcs.append(kc)
        scores.append(_dg(q_lat, kc, _NT) + _dg(q_pe, kr, _NN))
    for c in range(n_chain):
        update(c, scores[c], kcs[c])

    @pl.when(is_last)
    def _():
        wait_pages(b_nxt, j_nxt, 1 - slot)

    @pl.when(j == nj - 1)
    def _():
        cn = cn_ref[0]
        kn = kn_ref[0][:, :ROPE_DIM]
        sn = _dg(q_lat, cn, _NT) + _dg(q_pe, kn, _NT)
        sn = jnp.where(_iota2(sn.shape, 1) <= _iota2(sn.shape, 0) // H, sn, NEG_INF)
        update(0, sn, cn)
        m = m_ref[0]
        for c in range(1, n_chain):
            m = jnp.maximum(m, m_ref[c])
        acc = jnp.zeros(acc_ref.shape[1:], F32)
        l = jnp.zeros(l_ref.shape[1:], F32)
        for c in range(n_chain):
            wgt = jnp.exp(m_ref[c] - m)
            acc = acc + acc_ref[c] * wgt
            l = l + l_ref[c] * wgt
        o_ref[0] = (acc / l).astype(o_ref.dtype)


def paged_attn_call(qcat, ckv_new, kpe_new, cache_ckv, cache_kpe_t, page_table, *, H):
    Bd, R, QW = qcat.shape
    C_KV = QW - LANES
    n_pages = page_table.shape[1]
    P = _pick(n_pages, (32, 16, 8, 4, 2))
    n_chain = _pick(P, (4, 2, 1))
    page = cache_ckv.shape[1]

    in_specs = [pl.BlockSpec((1, R, QW), lambda b, j, pt: (b, 0, 0)),
                pl.BlockSpec((1, 8, C_KV), lambda b, j, pt: (b, 0, 0)),
                pl.BlockSpec((1, 8, LANES), lambda b, j, pt: (b, 0, 0)),
                pl.BlockSpec(memory_space=pl.ANY), pl.BlockSpec(memory_space=pl.ANY)]
    grid_spec = pltpu.PrefetchScalarGridSpec(
        num_scalar_prefetch=1, grid=(Bd, n_pages // P), in_specs=in_specs,
        out_specs=pl.BlockSpec((1, R, C_KV), lambda b, j, pt: (b, 0, 0)),
        scratch_shapes=[pltpu.VMEM((2, P, page, C_KV), F32), pltpu.VMEM((2, P, ROPE_DIM, page), F32),
                        pltpu.SemaphoreType.DMA((2, 2)),
                        pltpu.VMEM((n_chain, R, 1), F32), pltpu.VMEM((n_chain, R, 1), F32),
                        pltpu.VMEM((n_chain, R, C_KV), F32)])
    return pl.pallas_call(
        functools.partial(_paged_attn_kernel, P=P, C_KV=C_KV, H=H, n_chain=n_chain),
        grid_spec=grid_spec, out_shape=jax.ShapeDtypeStruct((Bd, R, C_KV), BF16),
        compiler_params=_cparams(("arbitrary", "arbitrary")), name="paged_attn",
    )(page_table, qcat, ckv_new, kpe_new, cache_ckv, cache_kpe_t)


def _unabsorb_kernel(o_ref, wv_ref, out_ref):
    out_ref[...] = jnp.dot(o_ref[...], wv_ref[0], preferred_element_type=F32).astype(out_ref.dtype)


def unabsorb_call(o_lat, wv, *, H):
    M = o_lat.shape[0]
    C_KV = wv.shape[1]
    return pl.pallas_call(
        _unabsorb_kernel, grid=(H,),
        in_specs=[pl.BlockSpec((M, C_KV), lambda h: (0, h)), pl.BlockSpec((1, C_KV, LANES), lambda h: (h, 0, 0))],
        out_specs=pl.BlockSpec((M, LANES), lambda h: (0, h)),
        out_shape=jax.ShapeDtypeStruct((M, H * LANES), BF16),
        compiler_params=_cparams(("parallel",)), name="unabsorb_o",
    )(o_lat, wv)


def _pad_cols(w, n):
    return w if w.shape[1] == n else jnp.pad(w, ((0, 0), (0, n - w.shape[1])))


def _pad_rows(w, n):
    return w if w.shape[0] == n else jnp.pad(w, ((0, n - w.shape[0]), (0, 0)))


def _round_up(n, m):
    return -(-n // m) * m


def _prep_weights(p):
    w = {"q_w_a": p["q_w_a"].astype(BF16)}
    for n1, n2 in (("rw_w1", "rw_w2"), ("rw_a1", "rw_a2"), ("rw_g1", "rw_g2")):
        R = _round_up(p[n1].shape[2], LANES)
        w[n1] = [_pad_cols(x, R).astype(BF16) for x in p[n1]]
        w[n2] = [_pad_rows(x, R).astype(BF16) for x in p[n2]]
    C_KV = p["kv_g_a"].shape[0]
    kv_w_a = p["kv_w_a"]
    w["kv_w_a"] = jnp.concatenate([kv_w_a, _rot_cols(kv_w_a[:, C_KV:])], axis=1).astype(BF16)
    H = p["kv_w_b"].shape[1]
    nope = p["kv_w_b"].shape[2] - LANES
    assert nope == LANES and H % 2 == 0 and kv_w_a.shape[1] == C_KV + ROPE_DIM
    Bn, QL, _ = p["q_w_b"].shape
    qb = p["q_w_b"].reshape(Bn, QL, H, nope + ROPE_DIM)
    qb = jnp.concatenate([qb, _rot_cols(qb[..., nope:])], axis=-1)
    w["q_w_b"] = [qb[l].reshape(QL, H * 256).astype(BF16) for l in range(Bn)]
    wk = p["kv_w_b"][..., :nope]
    wv = p["kv_w_b"][..., nope:]
    w["wk_cols"] = wk.reshape(C_KV, H * LANES).astype(BF16)
    w["wv_cols"] = wv.reshape(C_KV, H * LANES).astype(BF16)
    w["wk_t"] = jnp.transpose(wk, (1, 2, 0)).astype(BF16)
    w["wv_h"] = jnp.transpose(wv, (1, 0, 2)).astype(BF16)
    return w


class _BigWeights:
    def __init__(self, p):
        self.p = p
        self.copies = {}
        self.emitting = True

    def matmul(self, a, names, l, *, tn, tn_emit=None, **kw):
        if self.emitting:
            res = matmul_call(a, [(self.p[n], l) for n in names], tn=tn_emit or tn, emit_bf16=True, **kw)
            for n, c in zip(names, res[-len(names):]):
                self.copies[(n, l)] = c
            return res[:-len(names)]
        return matmul_call(a, [self.copies[(n, l)] for n in names], tn=tn, **kw)


def _ffn(x, mix, l, p, big, tm):
    x, h = norm_call(x, mix, p["g_mix_post"][l], [p["g_ffn_pre"][l]], [BF16])
    M = x.shape[0]
    F = p["ffn_w_gate"].shape[2]
    tm_gu = 2 * tm if M % (2 * tm) == 0 else tm
    act = big.matmul(h, ["ffn_w_gate", "ffn_w_up"], l, tm=tm_gu, tn=256, epilogue=_ep_swiglu, name="ffn_gate_up",
                     outs=lambda tm_, tn_: [((M, F), BF16, (tm_, tn_), lambda i, j: (i, j))])[0]
    y = big.matmul(act, ["ffn_w_down"], l, tm=min(tm, 512), tn=512, tn_emit=LANES, name="ffn_down")[0]
    return x, y


def _rwkv_layer(x, shift_prev, s0, p, w, big, *, B, T, tm, chunked):
    l = 0
    M, D = x.shape
    if chunked:
        m_r, m_w, m_k, m_v, m_a, m_g = norm_mix_call(x, p["g_mix_pre"][l], p["rw_mu"][l], seq_len=T)
        (shift_out,) = norm_call(x.reshape(B, T, D)[:, -1], pre_gains=[p["g_mix_pre"][l]], pre_dtypes=[F32])
    else:
        (h,) = norm_call(x, pre_gains=[p["g_mix_pre"][l]], pre_dtypes=[F32])
        h3 = h.reshape(B, T, D)
        hp = jnp.concatenate([shift_prev[:, None, :], h3[:, :-1]], axis=1).reshape(M, D)
        shift_out = h3[:, -1]
        m_r, m_w, m_k, m_v, m_a, m_g = mix_call(h, hp, p["rw_mu"][l])
    r = big.matmul(m_r, ["rw_w_r"], l, tm=tm, tn=1024, tn_emit=512, name="rwkv_r")[0]
    k = big.matmul(m_k, ["rw_w_k"], l, tm=tm, tn=1024, tn_emit=512, name="rwkv_k")[0]
    v = big.matmul(m_v, ["rw_w_v"], l, tm=tm, tn=1024, tn_emit=512, name="rwkv_v")[0]
    lw = lora_call(m_w, w["rw_w1"][l], w["rw_w2"][l], p["rw_w0"][l], "decay")
    a = lora_call(m_a, w["rw_a1"][l], w["rw_a2"][l], p["rw_a0"][l], "iclr")
    g = lora_call(m_g, w["rw_g1"][l], w["rw_g2"][l], jnp.zeros((D,), F32), "gate")
    pars = (p["rw_k_k"][l], p["rw_k_a"][l], p["rw_r_k"][l].reshape(D), p["rw_ln_w"][l], p["rw_ln_b"][l])
    if chunked:
        yg, s_new = wkv_chunk_call(r, k, v, lw, a, g, *pars, T=T)
    else:
        tmaj = lambda z: jnp.transpose(z.reshape(B, T, D), (1, 2, 0))
        yg_t, s_t = wkv_step_call(*[tmaj(z) for z in (r, k, v, lw, a, g)], *pars,
                                  jnp.transpose(s0, (1, 2, 3, 0)), T=T)
        yg = jnp.transpose(yg_t, (2, 0, 1)).reshape(M, D)
        s_new = jnp.transpose(s_t, (3, 0, 1, 2))
    mix = big.matmul(yg, ["rw_w_o"], l, tm=tm, tn=1024, tn_emit=512, name="rwkv_o")[0]
    return mix, s_new, shift_out


def _mla_common(x_h1, hkv, p, w, j, *, pos0, B, T, tm, scale):
    M = x_h1.shape[0]
    cos, sin = _rope_tables(pos0, T)
    cos = jnp.tile(cos, (B, 1))
    sin = jnp.tile(sin, (B, 1))
    ckv, kpe, ckv_b, kpe_b = kv_call(hkv, w["kv_w_a"], p["kv_g_a"], cos, sin)
    QL = w["q_w_a"].shape[2]
    tma = min(tm, 512, M)
    qa = matmul_call(x_h1, [(w["q_w_a"], j)], tm=tma, tn=QL, epilogue=_ep_rmsnorm, name="q_down",
                     extras=[(p["q_g_a"][j].reshape(1, QL), (1, QL), lambda i, jj: (0, 0))],
                     outs=[((M, QL), BF16, (tma, QL), lambda i, jj: (i, 0))])[0]
    HN = w["q_w_b"][j].shape[1]
    tmq = min(tm, M)
    tnq = _pick(HN, (1024, 512, 256))
    q = matmul_call(qa, [w["q_w_b"][j]], tm=tmq, tn=tnq, epilogue=functools.partial(_ep_rope_q, scale=scale),
                    name="q_up_rope",
                    extras=[(cos, (tmq, LANES), lambda i, jj: (i, 0)), (sin, (tmq, LANES), lambda i, jj: (i, 0))],
                    outs=[((M, HN), BF16, (tmq, tnq), lambda i, jj: (i, jj))])[0]
    return ckv, kpe, ckv_b, kpe_b, q


def kernel(x_prompt, x_sample, state_wkv, state_shift, cache_ckv, cache_kpe, page_table, g_mix_pre, g_mix_post,
           g_ffn_pre, g_ffn_post, rw_mu, rw_w_r, rw_w_k, rw_w_v, rw_w_o, rw_w0, rw_w1, rw_w2, rw_a0, rw_a1, rw_a2,
           rw_g1, rw_g2, rw_k_k, rw_k_a, rw_r_k, rw_ln_w, rw_ln_b, kv_g_in, kv_w_a, kv_g_a, kv_w_b, q_w_a, q_g_a,
           q_w_b, mla_w_o, ffn_w_gate, ffn_w_up, ffn_w_down):
    p = dict(g_mix_pre=g_mix_pre, g_mix_post=g_mix_post, g_ffn_pre=g_ffn_pre, g_ffn_post=g_ffn_post,
             rw_mu=rw_mu, rw_w_r=rw_w_r, rw_w_k=rw_w_k, rw_w_v=rw_w_v, rw_w_o=rw_w_o,
             rw_w0=rw_w0, rw_w1=rw_w1, rw_w2=rw_w2, rw_a0=rw_a0, rw_a1=rw_a1, rw_a2=rw_a2,
             rw_g1=rw_g1, rw_g2=rw_g2, rw_k_k=rw_k_k, rw_k_a=rw_k_a, rw_r_k=rw_r_k,
             rw_ln_w=rw_ln_w, rw_ln_b=rw_ln_b, kv_g_in=kv_g_in, kv_w_a=kv_w_a, kv_g_a=kv_g_a, kv_w_b=kv_w_b,
             q_w_a=q_w_a, q_g_a=q_g_a, q_w_b=q_w_b, mla_w_o=mla_w_o,
             ffn_w_gate=ffn_w_gate, ffn_w_up=ffn_w_up, ffn_w_down=ffn_w_down)
    assert g_mix_pre.shape[0] == 2 and rw_mu.shape[0] == 1 and q_w_a.shape[0] == 1, "one RWKV-7 + one MLA layer"
    assert cache_ckv.shape[1] == PAGE_SIZE
    w = _prep_weights(p)
    D = x_prompt.shape[-1]
    H = kv_w_b.shape[1]
    C_KV = kv_g_a.shape[0]
    scale = float((LANES + ROPE_DIM) ** -0.5)
    outs = {}
    big = _BigWeights(p)
    for grp in ("sample", "prompt"):
        big.emitting = grp == "sample"
        if grp == "prompt":
            x3 = x_prompt
            B, T, _ = x3.shape
            s0 = None
            shift0 = jnp.zeros((B, D), F32)
            pos0 = 0
        else:
            x3 = x_sample
            B, T, _ = x3.shape
            s0 = state_wkv[0]
            shift0 = state_shift[0]
            pos0 = page_table.shape[1] * PAGE_SIZE
        M = B * T
        tm = _pick(M, (1024, 512, 256, 128, 64, 32, 16, 8))
        x = x3.reshape(M, D)
        mix, s_new, shift_out = _rwkv_layer(x, shift0, s0, p, w, big, B=B, T=T, tm=tm, chunked=(grp == "prompt"))
        x, y = _ffn(x, mix, 0, p, big, tm)
        x, h1, hkv = norm_call(x, y, g_ffn_post[0], [g_mix_pre[1], kv_g_in], [BF16, BF16])
        ckv, kpe, ckv_b, kpe_b, q = _mla_common(h1, hkv, p, w, 0, pos0=pos0, B=B, T=T, tm=tm, scale=scale)
        if grp == "prompt":
            tme = min(tm, M)
            tne = _pick(H * LANES, (1024, 512, 256))
            k_all, v_all = matmul_call(
                ckv_b, [w["wk_cols"], w["wv_cols"]], tm=tme, tn=tne, epilogue=_ep_expand_kv, name="kv_expand",
                extras=[(kpe_b, (tme, LANES), lambda i, jj: (i, 0))],
                outs=[((M, H * 256), BF16, (tme, 2 * tne), lambda i, jj: (i, jj)),
                      ((M, H * LANES), BF16, (tme, tne), lambda i, jj: (i, jj))])
            o = attn_prompt_call(q, k_all, v_all, B=B, T=T, H=H)
        else:
            qcat = absorb_q_call(q, w["wk_t"], H=H).reshape(B, T * H, C_KV + LANES)
            pad8 = lambda z: jnp.pad(z.reshape(B, T, z.shape[-1]), ((0, 0), (0, 8 - T), (0, 0)))
            o_lat = paged_attn_call(qcat, pad8(ckv_b), pad8(kpe_b), cache_ckv, jnp.transpose(cache_kpe, (0, 2, 1)),
                                    page_table, H=H)
            o = unabsorb_call(o_lat.reshape(M, H * C_KV), w["wv_h"], H=H)
        mix = big.matmul(o, ["mla_w_o"], 0, tm=min(tm, 512), tn=512, tn_emit=256, name="mla_out")[0]
        x, y = _ffn(x, mix, 1, p, big, tm)
        (x,) = norm_call(x, y, g_ffn_post[1])
        outs[grp] = (x.reshape(B, T, D), s_new[None], shift_out[None], ckv.reshape(B, T, C_KV),
                     kpe.reshape(B, T, ROPE_DIM))
    yp, wp, sp, cp, kp = outs["prompt"]
    ys, ws, ss, cs, ks = outs["sample"]
    return (yp, ys, wp, sp, cp, kp, ws, ss, cs, ks)
```
